```python
import jax, jax.numpy as jnp
from jax import lax
import numpy as np

D_MODEL = 4096
BATCH = 1
SEQ = 16384
DEPTH = 2

CHUNK = 64

A_KEY = 128
A_VAL = 128
A_WIDTH = 3 * D_MODEL // 8
A_HEADS = A_WIDTH // A_VAL

B_WIDTH = D_MODEL // 4
B_HEAD_DIM = 128
B_HEADS = B_WIDTH // B_HEAD_DIM
B_ROT = B_HEAD_DIM // 4
IDX_HEADS = 16
IDX_DIM = 64
IDX_ROT = IDX_DIM // 4
IDX_TOPK_MAX = 256
Q_BLOCK = 128
ROPE_THETA = 500000.0

C_WIDTH = D_MODEL - A_WIDTH - B_WIDTH
C_VAL = 256
C_KEY = 128
C_HEADS = C_WIDTH // C_VAL
RET_THETA = 10000.0

MIX_WIDTH = A_WIDTH + B_WIDTH + C_WIDTH
PROJ_SIZES = (A_HEADS * A_KEY, A_HEADS * A_KEY, A_WIDTH, A_WIDTH,
              B_WIDTH, B_WIDTH, B_WIDTH, IDX_HEADS * IDX_DIM, IDX_DIM, IDX_HEADS,
              C_HEADS * C_KEY, C_HEADS * C_KEY, C_WIDTH, C_WIDTH)
PROJ_DIM = sum(PROJ_SIZES)

N_EXPERTS = 32
TOP_K = 4
D_EXPERT = 768
SWIGLU_LIMIT = 7.0
SWIGLU_ALPHA = 1.702
MOE_BLOCK = 256

DN_ALPHA = (2 * DEPTH) ** 0.25
DN_BETA = (8 * DEPTH) ** -0.25
LN_EPS = 1e-5
NORM_EPS = 1e-6

kernel_name = 'hybrid_hgrn2_dsa_retention_moe_deepnorm'

F32 = jnp.float32


def layer_norm(x, g, b):
    xf = x.astype(F32)
    mu = jnp.mean(xf, axis=-1, keepdims=True)
    var = jnp.mean(jnp.square(xf - mu), axis=-1, keepdims=True)
    return ((xf - mu) * lax.rsqrt(var + LN_EPS) * g.astype(F32) + b.astype(F32)).astype(x.dtype)


def rope_tables(seq, n_rot, theta):
    inv_freq = 1.0 / (theta ** (jnp.arange(0, n_rot, 2, dtype=F32) / n_rot))
    ang = jnp.arange(seq, dtype=F32)[:, None] * inv_freq[None, :]
    return jnp.cos(ang), jnp.sin(ang)


def apply_rope(x, cos, sin, n_rot):
    half = n_rot // 2
    x1, x2 = x[..., :half], x[..., half:n_rot]
    c, s = cos[:, None, :], sin[:, None, :]
    return jnp.concatenate([x1 * c - x2 * s, x2 * c + x1 * s, x[..., n_rot:]], axis=-1)


def to_chunks(t, n_heads, d):
    bsz, seq = t.shape[:2]
    return t.reshape(bsz, seq // CHUNK, CHUNK, n_heads, d).transpose(1, 0, 3, 2, 4)


def from_chunks(o):
    n, bsz, h, c, d = o.shape
    return o.transpose(1, 0, 3, 2, 4).reshape(bsz, n * c, h, d)


def hgrn2_mixer(q, f_logit, i, g, lb, norm_g):
    bsz, seq, _ = q.shape
    f = lb + (1.0 - lb) * jax.nn.sigmoid(f_logit.astype(F32))
    log_f = jnp.log(f)
    k = 1.0 - f
    qc = to_chunks(q.astype(F32) * A_KEY ** -0.5, A_HEADS, A_KEY)
    kc = to_chunks(k, A_HEADS, A_KEY)
    vc = to_chunks(i.astype(F32), A_HEADS, A_VAL)
    lfc = to_chunks(log_f, A_HEADS, A_KEY)
    causal = jnp.tril(jnp.ones((CHUNK, CHUNK), dtype=bool))[:, :, None]

    def step(state, inp):
        qb, kb, vb, lf = inp
        cum = jnp.cumsum(lf, axis=-2)
        inter = jnp.einsum('bhck,bhkv->bhcv', qb * jnp.exp(cum), state)
        diff = cum[:, :, :, None, :] - cum[:, :, None, :, :]
        dec = jnp.exp(jnp.where(causal, diff, -jnp.inf))
        att = jnp.sum(qb[:, :, :, None, :] * kb[:, :, None, :, :] * dec, axis=-1)
        intra = jnp.einsum('bhcs,bhsv->bhcv', att, vb)
        last = cum[:, :, -1:, :]
        state = (jnp.exp(last[:, :, 0, :])[..., None] * state
                 + jnp.einsum('bhsk,bhsv->bhkv', kb * jnp.exp(last - cum), vb))
        return state, inter + intra

    s0 = jnp.zeros((bsz, A_HEADS, A_KEY, A_VAL), F32)
    _, o = lax.scan(step, s0, (qc, kc, vc, lfc))
    o = from_chunks(o)
    o = o * lax.rsqrt(jnp.mean(jnp.square(o), axis=-1, keepdims=True) + NORM_EPS) * norm_g.astype(F32)
    o = o * jax.nn.silu(g.astype(F32)).reshape(bsz, seq, A_HEADS, A_VAL)
    return o.reshape(bsz, seq, A_WIDTH)


def retention_mixer(q, k, v, g, cos, sin, norm_g):
    bsz, seq, _ = q.shape
    q = apply_rope(q.astype(F32).reshape(bsz, seq, C_HEADS, C_KEY), cos, sin, C_KEY)
    k = apply_rope(k.astype(F32).reshape(bsz, seq, C_HEADS, C_KEY), cos, sin, C_KEY) * C_KEY ** -0.5
    qc = q.reshape(bsz, seq // CHUNK, CHUNK, C_HEADS, C_KEY).transpose(1, 0, 3, 2, 4)
    kc = k.reshape(bsz, seq // CHUNK, CHUNK, C_HEADS, C_KEY).transpose(1, 0, 3, 2, 4)
    vc = to_chunks(v.astype(F32), C_HEADS, C_VAL)
    log_gamma = jnp.log(1.0 - jnp.exp(jnp.linspace(jnp.log(1.0 / 32), jnp.log(1.0 / 512), C_HEADS)))
    pos = jnp.arange(CHUNK, dtype=F32)
    rel = pos[:, None] - pos[None, :]
    dmat = jnp.exp(jnp.where(rel >= 0, log_gamma[:, None, None] * rel, -jnp.inf))
    q_dec = jnp.exp(log_gamma[:, None] * (pos + 1.0))[..., None]
    k_dec = jnp.exp(log_gamma[:, None] * (CHUNK - 1.0 - pos))[..., None]
    c_dec = jnp.exp(log_gamma * CHUNK)[:, None, None]

    def step(state, inp):
        qb, kb, vb = inp
        intra = jnp.einsum('bhcs,bhsv->bhcv', jnp.einsum('bhcd,bhsd->bhcs', qb, kb) * dmat, vb)
        inter = jnp.einsum('bhcd,bhdv->bhcv', qb * q_dec, state)
        state = c_dec * state + jnp.einsum('bhsd,bhsv->bhdv', kb * k_dec, vb)
        return state, intra + inter

    s0 = jnp.zeros((bsz, C_HEADS, C_KEY, C_VAL), F32)
    _, o = lax.scan(step, s0, (qc, kc, vc))
    o = from_chunks(o)
    mu = jnp.mean(o, axis=-1, keepdims=True)
    var = jnp.mean(jnp.square(o - mu), axis=-1, keepdims=True)
    o = (o - mu) * lax.rsqrt(var + NORM_EPS) * norm_g.astype(F32)
    o = o * jax.nn.silu(g.astype(F32)).reshape(bsz, seq, C_HEADS, C_VAL)
    return o.reshape(bsz, seq, C_WIDTH)


def dsa_mixer(q, k, v, qi, ki, wi, cos_b, sin_b, cos_i, sin_i):
    bsz, seq, _ = q.shape
    q = apply_rope(q.astype(F32).reshape(bsz, seq, B_HEADS, B_HEAD_DIM), cos_b, sin_b, B_ROT)
    k = apply_rope(k.astype(F32).reshape(bsz, seq, B_HEADS, B_HEAD_DIM), cos_b, sin_b, B_ROT)
    v = v.astype(F32).reshape(bsz, seq, B_HEADS, B_HEAD_DIM)
    qi = apply_rope(qi.astype(F32).reshape(bsz, seq, IDX_HEADS, IDX_DIM), cos_i, sin_i, IDX_ROT)
    ki = apply_rope(ki.astype(F32)[:, :, None, :], cos_i, sin_i, IDX_ROT)[:, :, 0, :]
    wi = wi.astype(F32) * IDX_HEADS ** -0.5
    topk = min(IDX_TOPK_MAX, seq // 4)
    key_pos = jnp.arange(seq)

    def block(j):
        s0 = j * Q_BLOCK
        qb = lax.dynamic_slice_in_dim(q, s0, Q_BLOCK, axis=1)
        qib = lax.dynamic_slice_in_dim(qi, s0, Q_BLOCK, axis=1)
        wib = lax.dynamic_slice_in_dim(wi, s0, Q_BLOCK, axis=1)
        qpos = s0 + jnp.arange(Q_BLOCK)
        limit = (qpos // CHUNK + 1) * CHUNK
        visible = key_pos[None, :] < limit[:, None]
        idx_logits = jnp.einsum('bqhd,bsd->bqhs', qib, ki) * IDX_DIM ** -0.5
        score = jnp.einsum('bqhs,bqh->bqs', jax.nn.relu(idx_logits), wib)
        score = jnp.where(visible[None], score, -jnp.inf)
        _, sel = lax.top_k(score, topk)
        sel_ok = sel < limit[None, :, None]
        ks = jax.vmap(lambda kk, ii: kk[ii])(k, sel)
        vs = jax.vmap(lambda vv, ii: vv[ii])(v, sel)
        logits = jnp.einsum('bqhd,bqkhd->bhqk', qb, ks) * B_HEAD_DIM ** -0.5
        logits = jnp.where(sel_ok[:, None], logits, -jnp.inf)
        p = jax.nn.softmax(logits, axis=-1)
        return jnp.einsum('bhqk,bqkhd->bqhd', p, vs)

    o = lax.map(block, jnp.arange(seq // Q_BLOCK))
    return o.transpose(1, 0, 2, 3, 4).reshape(bsz, seq, B_WIDTH)


def moe_ffn(h, router_w, router_b, w_gate_up, b_gate_up, w_down, b_down):
    bsz, seq, d = h.shape
    n_tok = bsz * seq
    xt = h.reshape(n_tok, d)
    logits = (xt @ router_w).astype(F32) + router_b.astype(F32)
    top_val, top_idx = lax.top_k(logits, TOP_K)
    gates = jax.nn.softmax(top_val, axis=-1)
    flat_e = top_idx.reshape(-1)
    flat_t = jnp.repeat(jnp.arange(n_tok, dtype=jnp.int32), TOP_K)
    flat_g = gates.reshape(-1)
    order = jnp.argsort(flat_e, stable=True)
    se, st, sg = flat_e[order], flat_t[order], flat_g[order]
    counts = jnp.bincount(flat_e, length=N_EXPERTS)
    padded = (counts + MOE_BLOCK - 1) // MOE_BLOCK * MOE_BLOCK
    start = jnp.cumsum(counts) - counts
    pend = jnp.cumsum(padded)
    pstart = pend - padded
    n_assign = n_tok * TOP_K
    dest = pstart[se] + jnp.arange(n_assign, dtype=jnp.int32) - start[se]
    n_blocks = -(-n_assign // MOE_BLOCK) + N_EXPERTS
    n_rows = n_blocks * MOE_BLOCK
    row_tok = jnp.zeros((n_rows,), jnp.int32).at[dest].set(st)
    row_gate = jnp.zeros((n_rows,), F32).at[dest].set(sg)
    block_expert = jnp.minimum(
        jnp.searchsorted(pend, jnp.arange(n_blocks, dtype=pend.dtype) * MOE_BLOCK, side='right'),
        N_EXPERTS - 1)

    def expert_block(y, blk):
        tok, gate, e = blk
        gu = (xt[tok] @ w_gate_up[e] + b_gate_up[e]).astype(F32)
        g_lin, u_lin = jnp.split(gu, 2, axis=-1)
        g_lin = jnp.minimum(g_lin, SWIGLU_LIMIT)
        u_lin = jnp.clip(u_lin, -SWIGLU_LIMIT, SWIGLU_LIMIT)
        act = (u_lin + 1.0) * g_lin * jax.nn.sigmoid(SWIGLU_ALPHA * g_lin)
        out = (act.astype(xt.dtype) @ w_down[e] + b_down[e]).astype(F32)
        return y.at[tok].add(out * gate[:, None]), None

    y, _ = lax.scan(expert_block, jnp.zeros((n_tok, d), F32),
                    (row_tok.reshape(n_blocks, MOE_BLOCK), row_gate.reshape(n_blocks, MOE_BLOCK), block_expert))
    return y.reshape(bsz, seq, d).astype(h.dtype)


def setup_inputs(seed: int = 0) -> dict:
    key = jax.random.key(seed)
    ks = jax.random.split(key, 20)

    def nrm(k, shape, scale):
        return jax.random.normal(k, shape, F32) * scale

    L, D, E, F = DEPTH, D_MODEL, N_EXPERTS, D_EXPERT
    return {
        'x': nrm(ks[0], (BATCH, SEQ, D), 1.0),
        'ln_in_g': 1.0 + nrm(ks[1], (D,), 0.02),
        'ln_in_b': nrm(ks[2], (D,), 0.02),
        'w_in': nrm(ks[3], (L, D, PROJ_DIM), D ** -0.5),
        'w_out': nrm(ks[4], (L, MIX_WIDTH, D), MIX_WIDTH ** -0.5 * DN_BETA),
        'hgrn_lb': nrm(ks[5], (L, A_HEADS * A_KEY), 0.5),
        'hgrn_norm_g': 1.0 + nrm(ks[6], (L, A_HEADS, A_VAL), 0.02),
        'ret_norm_g': 1.0 + nrm(ks[7], (L, C_HEADS, C_VAL), 0.02),
        'ln1_g': 1.0 + nrm(ks[8], (L, D), 0.02),
        'ln1_b': nrm(ks[9], (L, D), 0.02),
        'router_w': nrm(ks[10], (L, D, E), D ** -0.5),
        'router_b': nrm(ks[11], (L, E), 0.01),
        'w_gate_up': nrm(ks[12], (L, E, D, 2 * F), D ** -0.5),
        'b_gate_up': nrm(ks[13], (L, E, 2 * F), 0.01),
        'w_down': nrm(ks[14], (L, E, F, D), F ** -0.5 * DN_BETA),
        'b_down': nrm(ks[15], (L, E, D), 0.01),
        'ln2_g': 1.0 + nrm(ks[16], (L, D), 0.02),
        'ln2_b': nrm(ks[17], (L, D), 0.02),
    }


def reference(x, ln_in_g, ln_in_b, w_in, w_out, hgrn_lb, hgrn_norm_g, ret_norm_g, ln1_g, ln1_b,
              router_w, router_b, w_gate_up, b_gate_up, w_down, b_down, ln2_g, ln2_b):
    seq = x.shape[1]
    cos_b, sin_b = rope_tables(seq, B_ROT, ROPE_THETA)
    cos_i, sin_i = rope_tables(seq, IDX_ROT, ROPE_THETA)
    cos_r, sin_r = rope_tables(seq, C_KEY, RET_THETA)
    lb_all = jnp.cumsum(jax.nn.softmax(hgrn_lb.astype(F32), axis=0), axis=0)
    lb_all = lb_all - lb_all[0:1]
    splits = np.cumsum(PROJ_SIZES)[:-1].tolist()
    h = layer_norm(x, ln_in_g, ln_in_b)
    for l in range(DEPTH):
        p = h @ w_in[l]
        aq, af, ai, ag, bq, bk, bv, iq, ik, iw, cq, ck, cv, cg = jnp.split(p, splits, axis=-1)
        oa = hgrn2_mixer(aq, af, ai, ag, lb_all[l], hgrn_norm_g[l])
        ob = dsa_mixer(bq, bk, bv, iq, ik, iw, cos_b, sin_b, cos_i, sin_i)
        oc = retention_mixer(cq, ck, cv, cg, cos_r, sin_r, ret_norm_g[l])
        mix = jnp.concatenate([oa, ob, oc], axis=-1).astype(h.dtype) @ w_out[l]
        h = layer_norm(DN_ALPHA * h + mix, ln1_g[l], ln1_b[l])
        ffn = moe_ffn(h, router_w[l], router_b[l], w_gate_up[l], b_gate_up[l], w_down[l], b_down[l])
        h = layer_norm(DN_ALPHA * h + ffn, ln2_g[l], ln2_b[l])
    return h
```

```python
import functools
import math

import numpy as np
import jax
import jax.numpy as jnp
from jax import lax
from jax.experimental import pallas as pl
from jax.experimental.pallas import tpu as pltpu

F32 = jnp.float32
BF16 = jnp.bfloat16
HI = lax.Precision.HIGHEST

LANES = 128
VMEM_LIMIT = 56 * 1024 * 1024

D_MODEL = 4096
DEPTH = 2
CHUNK = 64

A_KEY = 128
A_VAL = 128
A_WIDTH = 1536
A_HEADS = 12

B_WIDTH = 1024
B_HEAD_DIM = 128
B_HEADS = 8
B_ROT = 32
IDX_HEADS = 16
IDX_DIM = 64
IDX_ROT = 16
IDX_TOPK_MAX = 256
ROPE_THETA = 500000.0

C_WIDTH = 1536
C_VAL = 256
C_KEY = 128
C_HEADS = 6
RET_THETA = 10000.0

N_EXPERTS = 32
TOP_K = 4
D_EXPERT = 768
SWIGLU_LIMIT = 7.0
SWIGLU_ALPHA = 1.702
MOE_BLOCK = 256

DN_ALPHA = (2 * DEPTH) ** 0.25
LN_EPS = 1e-5
NORM_EPS = 1e-6

OFF_AQ, OFF_AF, OFF_AI, OFF_AG = 0, 12, 24, 36
OFF_BQ, OFF_BK, OFF_BV, OFF_IQ = 48, 56, 64, 72
OFF_CQ, OFF_CK, OFF_CV, OFF_CG, OFF_IKW = 80, 86, 92, 104, 116
PROJ_BLOCKS = 117
PROJ_PAD = PROJ_BLOCKS * LANES
IKW_START, IKW_END = 10240, 10320

INT_MIN = -2 ** 31


def _cparams(sem, vmem=VMEM_LIMIT):
    return pltpu.CompilerParams(dimension_semantics=sem, vmem_limit_bytes=vmem)


def _ln_rows(x, g, b):
    mu = jnp.mean(x, axis=-1, keepdims=True)
    xc = x - mu
    var = jnp.mean(xc * xc, axis=-1, keepdims=True)
    return xc * lax.rsqrt(var + LN_EPS) * g + b


def _ln_in_kernel(x_ref, g_ref, b_ref, o_ref, ob_ref):
    y = _ln_rows(x_ref[...], g_ref[...], b_ref[...])
    o_ref[...] = y
    ob_ref[...] = y.astype(BF16)


def ln_in(x, g, b, rows=256):
    t, d = x.shape
    rows = min(rows, t)
    row_spec = pl.BlockSpec((rows, d), lambda i: (i, 0))
    vec_spec = pl.BlockSpec((1, d), lambda i: (0, 0))
    return pl.pallas_call(
        _ln_in_kernel,
        grid=(t // rows,),
        in_specs=[row_spec, vec_spec, vec_spec],
        out_specs=[row_spec, row_spec],
        out_shape=[jax.ShapeDtypeStruct((t, d), F32), jax.ShapeDtypeStruct((t, d), BF16)],
        compiler_params=_cparams(("parallel",)),
        name="ln_in",
    )(x, g.reshape(1, d), b.reshape(1, d))


def _ln1_router_kernel(h_ref, mm_ref, g_ref, b_ref, rw_ref, rb_ref, o_ref, lg_ref):
    y = _ln_rows(DN_ALPHA * h_ref[...] + mm_ref[...], g_ref[...], b_ref[...])
    o_ref[...] = y
    lg_ref[...] = jnp.dot(y.astype(BF16), rw_ref[...], preferred_element_type=F32) + rb_ref[...]


def ln1_router(h, mm, g, b, rw, rb, rows=256):
    t, d = h.shape
    rows = min(rows, t)
    row_spec = pl.BlockSpec((rows, d), lambda i: (i, 0))
    vec_spec = pl.BlockSpec((1, d), lambda i: (0, 0))
    return pl.pallas_call(
        _ln1_router_kernel,
        grid=(t // rows,),
        in_specs=[row_spec, row_spec, vec_spec, vec_spec,
                  pl.BlockSpec((d, LANES), lambda i: (0, 0)),
                  pl.BlockSpec((1, LANES), lambda i: (0, 0))],
        out_specs=[row_spec, pl.BlockSpec((rows, LANES), lambda i: (i, 0))],
        out_shape=[jax.ShapeDtypeStruct((t, d), F32), jax.ShapeDtypeStruct((t, LANES), F32)],
        compiler_params=_cparams(("parallel",)),
        name="ln1_router",
    )(h, mm, g.reshape(1, d), b.reshape(1, d), rw, rb)


def _matmul_kernel(a_ref, b_ref, o_ref):
    o_ref[...] = jnp.dot(a_ref[...], b_ref[...], preferred_element_type=F32)


def matmul(a, b, tm, tn):
    m, k = a.shape
    n = b.shape[1]
    tm = min(tm, m)
    return pl.pallas_call(
        _matmul_kernel,
        grid=(n // tn, m // tm),
        in_specs=[pl.BlockSpec((tm, k), lambda j, i: (i, 0)),
                  pl.BlockSpec((k, tn), lambda j, i: (0, j))],
        out_specs=pl.BlockSpec((tm, tn), lambda j, i: (i, j)),
        out_shape=jax.ShapeDtypeStruct((m, n), F32),
        compiler_params=_cparams(("parallel", "parallel")),
        name="matmul",
    )(a, b)


def _hgrn_kernel(q_ref, f_ref, i_ref, g_ref, lb_ref, ng_ref, o_ref,
                 state_ref, cum_ref, kk_ref, acc_ref, *, n_chunks):
    @pl.when(pl.program_id(1) == 0)
    def _():
        state_ref[...] = jnp.zeros_like(state_ref)

    lb = lb_ref[...]
    ng = ng_ref[...]
    row = lax.broadcasted_iota(jnp.int32, (CHUNK, CHUNK), 0)
    col = lax.broadcasted_iota(jnp.int32, (CHUNK, CHUNK), 1)
    tri = (col <= row).astype(F32)
    sub = 8

    for c in range(n_chunks):
        r0 = c * CHUNK
        f = lb + (1.0 - lb) * jax.nn.sigmoid(f_ref[r0:r0 + CHUNK, :])
        kk = 1.0 - f
        cum = jnp.dot(tri, jnp.log(f), precision=HI, preferred_element_type=F32)
        qs = q_ref[r0:r0 + CHUNK, :] * (A_KEY ** -0.5)
        inter = lax.dot_general(qs * jnp.exp(cum), state_ref[...], (((1,), (1,)), ((), ())),
                                precision=HI, preferred_element_type=F32)
        cum_ref[...] = cum
        kk_ref[...] = kk
        acc_ref[...] = inter

        for sb in range(CHUNK // sub):
            t0 = sb * sub
            qs_t = qs[t0:, :]
            cum_t = cum[t0:, :]
            t_idx = t0 + lax.broadcasted_iota(jnp.int32, (CHUNK - t0, 1), 0)

            def body(s, acc_t, qs_t=qs_t, cum_t=cum_t, t_idx=t_idx, r0=r0):
                rc = cum_ref[pl.ds(s, 1), :]
                rk = kk_ref[pl.ds(s, 1), :]
                rv = i_ref[pl.ds(r0 + s, 1), :]
                dec = jnp.exp(jnp.minimum(cum_t - rc, 0.0))
                a = jnp.sum(qs_t * rk * dec, axis=1, keepdims=True)
                a = jnp.where(t_idx >= s, a, 0.0)
                return acc_t + a * rv

            acc_t = lax.fori_loop(t0, t0 + sub, body, jnp.zeros((CHUNK - t0, A_VAL), F32), unroll=True)
            acc_ref[t0:, :] += acc_t

        last = cum[CHUNK - 1:CHUNK, :]
        kd = kk * jnp.exp(last - cum)
        v = i_ref[r0:r0 + CHUNK, :]
        state_ref[...] = state_ref[...] * jnp.exp(last) + lax.dot_general(
            v, kd, (((0,), (0,)), ((), ())), precision=HI, preferred_element_type=F32)

        o = acc_ref[...]
        o = o * lax.rsqrt(jnp.mean(o * o, axis=-1, keepdims=True) + NORM_EPS) * ng
        g = g_ref[r0:r0 + CHUNK, :]
        o_ref[r0:r0 + CHUNK, :] = (o * (g * jax.nn.sigmoid(g))).astype(o_ref.dtype)


def hgrn_mixer(p, lb, ng, rows=256):
    t = p.shape[0]
    rows = min(rows, t)

    def seg(off):
        return pl.BlockSpec((rows, LANES), lambda h, r: (r, off + h))

    vec = pl.BlockSpec((1, LANES), lambda h, r: (0, h))
    return pl.pallas_call(
        functools.partial(_hgrn_kernel, n_chunks=rows // CHUNK),
        grid=(A_HEADS, t // rows),
        in_specs=[seg(OFF_AQ), seg(OFF_AF), seg(OFF_AI), seg(OFF_AG), vec, vec],
        out_specs=pl.BlockSpec((rows, LANES), lambda h, r: (r, h)),
        out_shape=jax.ShapeDtypeStruct((t, A_WIDTH), BF16),
        scratch_shapes=[pltpu.VMEM((A_VAL, A_KEY), F32), pltpu.VMEM((CHUNK, A_KEY), F32),
                        pltpu.VMEM((CHUNK, A_KEY), F32), pltpu.VMEM((CHUNK, A_VAL), F32)],
        compiler_params=_cparams(("parallel", "arbitrary")),
        name="hgrn_mixer",
    )(p, p, p, p, lb, ng)


def _ret_consts():
    lg = np.log(1.0 - np.exp(np.linspace(np.log(1.0 / 32), np.log(1.0 / 512), C_HEADS)))
    pos = np.arange(CHUNK, dtype=np.float64)
    rel = pos[:, None] - pos[None, :]
    dmat = np.where(rel >= 0, np.exp(lg[:, None, None] * np.maximum(rel, 0.0)), 0.0)
    q_dec = np.broadcast_to(np.exp(lg[:, None] * (pos + 1.0))[..., None], (C_HEADS, CHUNK, C_KEY))
    k_dec = np.broadcast_to(np.exp(lg[:, None] * (CHUNK - 1.0 - pos))[..., None], (C_HEADS, CHUNK, C_KEY))
    c_dec = np.broadcast_to(np.exp(lg * CHUNK)[:, None, None], (C_HEADS, 1, C_VAL))
    return tuple(jnp.asarray(np.ascontiguousarray(a), F32) for a in (dmat, q_dec, k_dec, c_dec))


def _ret_kernel(q_ref, k_ref, v_ref, g_ref, cos_ref, sin_ref, dm_ref, qd_ref, kd_ref, cd_ref, ng_ref,
                o_ref, state_ref, *, n_chunks):
    @pl.when(pl.program_id(1) == 0)
    def _():
        state_ref[...] = jnp.zeros_like(state_ref)

    dmat = dm_ref[...]
    q_dec = qd_ref[...]
    k_dec = kd_ref[...]
    c_dec = cd_ref[...]
    ng = ng_ref[...]
    for c in range(n_chunks):
        r0 = c * CHUNK
        cos = cos_ref[r0:r0 + CHUNK, :]
        sin = sin_ref[r0:r0 + CHUNK, :]
        q = q_ref[r0:r0 + CHUNK, :]
        k = k_ref[r0:r0 + CHUNK, :]
        q = q * cos + pltpu.roll(q, C_KEY // 2, 1) * sin
        k = (k * cos + pltpu.roll(k, C_KEY // 2, 1) * sin) * (C_KEY ** -0.5)
        v = v_ref[r0:r0 + CHUNK, :]
        s = lax.dot_general(q, k, (((1,), (1,)), ((), ())), precision=HI, preferred_element_type=F32)
        intra = jnp.dot(s * dmat, v, precision=HI, preferred_element_type=F32)
        inter = jnp.dot(q * q_dec, state_ref[...], precision=HI, preferred_element_type=F32)
        state_ref[...] = c_dec * state_ref[...] + lax.dot_general(
            k * k_dec, v, (((0,), (0,)), ((), ())), precision=HI, preferred_element_type=F32)
        o = intra + inter
        mu = jnp.mean(o, axis=-1, keepdims=True)
        oc = o - mu
        var = jnp.mean(oc * oc, axis=-1, keepdims=True)
        o = oc * lax.rsqrt(var + NORM_EPS) * ng
        g = g_ref[r0:r0 + CHUNK, :]
        o_ref[r0:r0 + CHUNK, :] = (o * (g * jax.nn.sigmoid(g))).astype(o_ref.dtype)


def retention_mixer(p, cos, sin, ng, rows=256):
    t = p.shape[0]
    rows = min(rows, t)
    dmat, q_dec, k_dec, c_dec = _ret_consts()
    tab = pl.BlockSpec((rows, C_KEY), lambda h, r: (r, 0))
    return pl.pallas_call(
        functools.partial(_ret_kernel, n_chunks=rows // CHUNK),
        grid=(C_HEADS, t // rows),
        in_specs=[pl.BlockSpec((rows, C_KEY), lambda h, r: (r, OFF_CQ + h)),
                  pl.BlockSpec((rows, C_KEY), lambda h, r: (r, OFF_CK + h)),
                  pl.BlockSpec((rows, C_VAL), lambda h, r: (r, OFF_CV // 2 + h)),
                  pl.BlockSpec((rows, C_VAL), lambda h, r: (r, OFF_CG // 2 + h)),
                  tab, tab,
                  pl.BlockSpec((None, CHUNK, CHUNK), lambda h, r: (h, 0, 0)),
                  pl.BlockSpec((None, CHUNK, C_KEY), lambda h, r: (h, 0, 0)),
                  pl.BlockSpec((None, CHUNK, C_KEY), lambda h, r: (h, 0, 0)),
                  pl.BlockSpec((None, 1, C_VAL), lambda h, r: (h, 0, 0)),
                  pl.BlockSpec((1, C_VAL), lambda h, r: (0, h))],
        out_specs=pl.BlockSpec((rows, C_VAL), lambda h, r: (r, h)),
        out_shape=jax.ShapeDtypeStruct((t, C_WIDTH), BF16),
        scratch_shapes=[pltpu.VMEM((C_KEY, C_VAL), F32)],
        compiler_params=_cparams(("parallel", "arbitrary")),
        name="retention_mixer",
    )(p, p, p, p, cos, sin, dmat, q_dec, k_dec, c_dec, ng)


def _rope_tables(t, n_rot, theta, width):
    half = n_rot // 2
    inv_freq = 1.0 / (theta ** (jnp.arange(0, n_rot, 2, dtype=F32) / n_rot))
    ang = jnp.arange(t, dtype=F32)[:, None] * inv_freq[None, :]
    c, s = jnp.cos(ang), jnp.sin(ang)
    z = jnp.zeros((t, width - n_rot), F32)
    zh = jnp.zeros((t, half), F32)
    cos = jnp.concatenate([c, c, jnp.ones((t, width - n_rot), F32)], axis=1)
    sin_lo = jnp.concatenate([-s, zh, z], axis=1)
    sin_hi = jnp.concatenate([zh, s, z], axis=1)
    rep = LANES // width
    return tuple(jnp.tile(a, (1, rep)) for a in (cos, sin_lo, sin_hi))


def _rope128(x, cos, sin_lo, sin_hi, half):
    return x * cos + pltpu.roll(x, LANES - half, 1) * sin_lo + pltpu.roll(x, half, 1) * sin_hi


def _dsa_prep_kernel(q_ref, k_ref, v_ref, iq_ref, ikw_ref, cb_ref, slb_ref, shb_ref, ci_ref, sli_ref, shi_ref,
                     qo_ref, ko_ref, vo_ref, iqo_ref, iko_ref, wo_ref):
    cb, slb, shb = cb_ref[...], slb_ref[...], shb_ref[...]
    ci, sli, shi = ci_ref[...], sli_ref[...], shi_ref[...]
    for h in range(B_HEADS):
        sl = slice(h * LANES, (h + 1) * LANES)
        qo_ref[:, sl] = _rope128(q_ref[:, sl], cb, slb, shb, B_ROT // 2).astype(BF16)
        ko_ref[:, sl] = _rope128(k_ref[:, sl], cb, slb, shb, B_ROT // 2).astype(BF16)
        iqo_ref[:, sl] = _rope128(iq_ref[:, sl], ci, sli, shi, IDX_ROT // 2).astype(BF16)
    vo_ref[...] = v_ref[...].astype(BF16)
    ikw = ikw_ref[...]
    iko_ref[...] = _rope128(ikw, ci, sli, shi, IDX_ROT // 2)[:, :IDX_DIM].astype(BF16)
    wo_ref[...] = ikw[:, IDX_DIM:IDX_DIM + IDX_HEADS] * (IDX_DIM ** -0.5 * IDX_HEADS ** -0.5)


def dsa_prep(p, rows=256):
    t = p.shape[0]
    rows = min(rows, t)
    tabs_b = _rope_tables(t, B_ROT, ROPE_THETA, B_HEAD_DIM)
    tabs_i = _rope_tables(t, IDX_ROT, ROPE_THETA, IDX_DIM)
    wide = lambda off: pl.BlockSpec((rows, B_WIDTH), lambda r: (r, off // 8))
    tab = pl.BlockSpec((rows, LANES), lambda r: (r, 0))
    out_wide = pl.BlockSpec((rows, B_WIDTH), lambda r: (r, 0))
    return pl.pallas_call(
        _dsa_prep_kernel,
        grid=(t // rows,),
        in_specs=[wide(OFF_BQ), wide(OFF_BK), wide(OFF_BV), wide(OFF_IQ),
                  pl.BlockSpec((rows, LANES), lambda r: (r, OFF_IKW))] + [tab] * 6,
        out_specs=[out_wide, out_wide, out_wide, out_wide,
                   pl.BlockSpec((rows, IDX_DIM), lambda r: (r, 0)),
                   pl.BlockSpec((rows, IDX_HEADS), lambda r: (r, 0))],
        out_shape=[jax.ShapeDtypeStruct((t, B_WIDTH), BF16)] * 4
        + [jax.ShapeDtypeStruct((t, IDX_DIM), BF16), jax.ShapeDtypeStruct((t, IDX_HEADS), F32)],
        compiler_params=_cparams(("parallel",)),
        name="dsa_prep",
    )(p, p, p, p, p, *tabs_b, *tabs_i)


def _dsa_kernel(q_ref, qi_ref, w_ref, kit_ref, k_ref, v_ref, o_ref,
                keys_ref, thr_ref, m_ref, l_ref, acc_ref, *, qb, kb, topk):
    i = pl.program_id(0)
    j = pl.program_id(1)
    nkb = ((i + 1) * qb + kb - 1) // kb

    @pl.when(j == 0)
    def _():
        q_pos = i * qb + lax.broadcasted_iota(jnp.int32, (qb, 1), 0)
        limit = (q_pos // CHUNK + 1) * CHUNK
        w = w_ref[...]

        def score_block(b, carry):
            off = pl.multiple_of(b * kb, kb)
            kit = kit_ref[:, pl.ds(off, kb)]
            sc = jnp.zeros((qb, kb), F32)
            for h in range(IDX_HEADS):
                lg = jnp.dot(qi_ref[:, h * IDX_DIM:(h + 1) * IDX_DIM], kit, preferred_element_type=F32)
                sc = sc + jnp.maximum(lg, 0.0) * w[:, h:h + 1]
            bits = pltpu.bitcast(sc, jnp.int32)
            key = jnp.where(bits < 0, bits ^ jnp.int32(0x7FFFFFFF), bits)
            k_pos = off + lax.broadcasted_iota(jnp.int32, (1, kb), 1)
            keys_ref[:, pl.ds(off, kb)] = jnp.where(k_pos < limit, key, jnp.int32(INT_MIN))
            return carry

        lax.fori_loop(0, nkb, score_block, 0)

        def bit_step(it, thr):
            cand = thr ^ lax.shift_left(jnp.int32(1), 31 - it)
            cand_b = jnp.broadcast_to(cand, (qb, LANES))

            def count_block(b, cnt):
                off = pl.multiple_of(b * kb, kb)
                for c in range(kb // LANES):
                    blk = keys_ref[:, pl.ds(off + c * LANES, LANES)]
                    cnt = cnt + (blk >= cand_b).astype(jnp.int32)
                return cnt

            cnt = lax.fori_loop(0, nkb, count_block, jnp.zeros((qb, LANES), jnp.int32))
            tot = jnp.sum(cnt, axis=1, keepdims=True)
            return jnp.where(tot >= topk, cand, thr)

        thr = lax.fori_loop(0, 32, bit_step, jnp.full((qb, 1), INT_MIN, jnp.int32))
        thr_ref[...] = jnp.maximum(thr, jnp.int32(INT_MIN + 1))
        m_ref[...] = jnp.full_like(m_ref, -1e30)
        l_ref[...] = jnp.zeros_like(l_ref)
        acc_ref[...] = jnp.zeros_like(acc_ref)

    @pl.when(j < nkb)
    def _():
        off = pl.multiple_of(j * kb, kb)
        mask = keys_ref[:, pl.ds(off, kb)] >= thr_ref[...]
        for h in range(B_HEADS):
            sl = slice(h * B_HEAD_DIM, (h + 1) * B_HEAD_DIM)
            s = lax.dot_general(q_ref[:, sl], k_ref[:, sl], (((1,), (1,)), ((), ())),
                                preferred_element_type=F32) * (B_HEAD_DIM ** -0.5)
            s = jnp.where(mask, s, -1e30)
            m_old = m_ref[h]
            m_new = jnp.maximum(m_old, jnp.max(s, axis=1, keepdims=True))
            alpha = jnp.exp(m_old - m_new)
            p = jnp.where(mask, jnp.exp(s - m_new), 0.0)
            l_ref[h] = alpha * l_ref[h] + jnp.sum(p, axis=1, keepdims=True)
            acc_ref[:, sl] = alpha * acc_ref[:, sl] + jnp.dot(p.astype(BF16), v_ref[:, sl],
                                                              preferred_element_type=F32)
            m_ref[h] = m_new

    @pl.when(j == nkb - 1)
    def _():
        for h in range(B_HEADS):
            sl = slice(h * B_HEAD_DIM, (h + 1) * B_HEAD_DIM)
            o_ref[:, sl] = (acc_ref[:, sl] / l_ref[h]).astype(o_ref.dtype)


def dsa_mixer(p, qb=256, kb=1024):
    t = p.shape[0]
    qb = min(qb, t)
    kb = min(kb, t)
    topk = min(IDX_TOPK_MAX, t // 4)
    q, k, v, qi, ki, wi = dsa_prep(p)
    kit = ki.T

    def kv_map(i, j):
        return (jnp.minimum(j, ((i + 1) * qb + kb - 1) // kb - 1), 0)

    q_spec = pl.BlockSpec((qb, B_WIDTH), lambda i, j: (i, 0))
    return pl.pallas_call(
        functools.partial(_dsa_kernel, qb=qb, kb=kb, topk=topk),
        grid=(t // qb, t // kb),
        in_specs=[q_spec, q_spec,
                  pl.BlockSpec((qb, IDX_HEADS), lambda i, j: (i, 0)),
                  pl.BlockSpec((IDX_DIM, t), lambda i, j: (0, 0)),
                  pl.BlockSpec((kb, B_WIDTH), kv_map),
                  pl.BlockSpec((kb, B_WIDTH), kv_map)],
        out_specs=q_spec,
        out_shape=jax.ShapeDtypeStruct((t, B_WIDTH), BF16),
        scratch_shapes=[pltpu.VMEM((qb, t), jnp.int32), pltpu.VMEM((qb, 1), jnp.int32),
                        pltpu.VMEM((B_HEADS, qb, 1), F32), pltpu.VMEM((B_HEADS, qb, 1), F32),
                        pltpu.VMEM((qb, B_WIDTH), F32)],
        compiler_params=_cparams(("arbitrary", "arbitrary")),
        name="dsa_mixer",
    )(q, qi, wi, kit, k, v)


def _gather_rows_kernel(tok_ref, x_hbm, o_ref, sem, *, rows):
    base = pl.program_id(0) * rows

    def row_copy(r, src_row):
        return pltpu.make_async_copy(x_hbm.at[pl.ds(src_row, 1), :], o_ref.at[pl.ds(r, 1), :], sem)

    def issue(r, carry):
        row_copy(r, tok_ref[base + r]).start()
        return carry

    def drain(r, carry):
        row_copy(r, 0).wait()
        return carry

    lax.fori_loop(0, rows, issue, 0)
    lax.fori_loop(0, rows, drain, 0)


def gather_rows(x, row_tok, rows=MOE_BLOCK):
    n_rows = row_tok.shape[0]
    d = x.shape[1]
    return pl.pallas_call(
        functools.partial(_gather_rows_kernel, rows=rows),
        grid_spec=pltpu.PrefetchScalarGridSpec(
            num_scalar_prefetch=1,
            grid=(n_rows // rows,),
            in_specs=[pl.BlockSpec(memory_space=pl.ANY)],
            out_specs=pl.BlockSpec((rows, d), lambda b, tok: (b, 0)),
            scratch_shapes=[pltpu.SemaphoreType.DMA(())]),
        out_shape=jax.ShapeDtypeStruct((n_rows, d), x.dtype),
        compiler_params=_cparams(("arbitrary",)),
        name="moe_gather",
    )(row_tok, x)


def _expert_kernel(be_ref, x_ref, wg_ref, wu_ref, bg_ref, bu_ref, wd_ref, bd_ref, gate_ref, o_ref):
    f = pl.program_id(1)
    x = x_ref[...].astype(BF16)
    g_lin = jnp.dot(x, wg_ref[...], preferred_element_type=F32) + bg_ref[...]
    u_lin = jnp.dot(x, wu_ref[...], preferred_element_type=F32) + bu_ref[...]
    g_lin = jnp.minimum(g_lin, SWIGLU_LIMIT)
    u_lin = jnp.clip(u_lin, -SWIGLU_LIMIT, SWIGLU_LIMIT)
    act = (u_lin + 1.0) * g_lin * jax.nn.sigmoid(SWIGLU_ALPHA * g_lin)
    part = jnp.dot(act.astype(BF16), wd_ref[...], preferred_element_type=F32)

    @pl.when(f == 0)
    def _():
        o_ref[...] = part

    @pl.when(f == pl.num_programs(1) - 1)
    def _():
        o_ref[...] = (o_ref[...] + part + bd_ref[...]) * gate_ref[...]


def expert_ffn(xs, block_expert, row_gate, w_gu, b_gu, w_d, b_d, f_tiles=2):
    n_rows, d = xs.shape
    ft = D_EXPERT // f_tiles
    assert f_tiles == 2
    return pl.pallas_call(
        _expert_kernel,
        grid_spec=pltpu.PrefetchScalarGridSpec(
            num_scalar_prefetch=1,
            grid=(n_rows // MOE_BLOCK, f_tiles),
            in_specs=[pl.BlockSpec((MOE_BLOCK, d), lambda b, f, be: (b, 0)),
                      pl.BlockSpec((None, d, ft), lambda b, f, be: (be[b], 0, f)),
                      pl.BlockSpec((None, d, ft), lambda b, f, be: (be[b], 0, f_tiles + f)),
                      pl.BlockSpec((None, 1, ft), lambda b, f, be: (be[b], 0, f)),
                      pl.BlockSpec((None, 1, ft), lambda b, f, be: (be[b], 0, f_tiles + f)),
                      pl.BlockSpec((None, ft, d), lambda b, f, be: (be[b], f, 0)),
                      pl.BlockSpec((None, 1, d), lambda b, f, be: (be[b], 0, 0)),
                      pl.BlockSpec((MOE_BLOCK, 1), lambda b, f, be: (b, 0))],
            out_specs=pl.BlockSpec((MOE_BLOCK, d), lambda b, f, be: (b, 0))),
        out_shape=jax.ShapeDtypeStruct((n_rows, d), F32),
        compiler_params=_cparams(("arbitrary", "arbitrary")),
        name="moe_experts",
    )(block_expert, xs, w_gu, w_gu, b_gu, b_gu, w_d, b_d, row_gate)


def _combine_ln_kernel(pos_ref, ys_hbm, h_ref, g_ref, b_ref, o_ref, ob_ref, buf_ref, sem, *, rows):
    base = pl.program_id(0) * rows

    def row_copy(r, k, src_row):
        return pltpu.make_async_copy(ys_hbm.at[pl.ds(src_row, 1), :], buf_ref.at[k, pl.ds(r, 1), :], sem)

    def issue(r, carry):
        for k in range(TOP_K):
            row_copy(r, k, pos_ref[(base + r) * TOP_K + k]).start()
        return carry

    def drain(r, carry):
        for k in range(TOP_K):
            row_copy(r, k, 0).wait()
        return carry

    lax.fori_loop(0, rows, issue, 0)
    lax.fori_loop(0, rows, drain, 0)
    ffn = (buf_ref[0] + buf_ref[1]) + (buf_ref[2] + buf_ref[3])
    y = _ln_rows(DN_ALPHA * h_ref[...] + ffn, g_ref[...], b_ref[...])
    o_ref[...] = y
    ob_ref[...] = y.astype(BF16)


def combine_ln(ys, pos, h, g, b, rows=64):
    t, d = h.shape
    rows = min(rows, t)
    row_spec = pl.BlockSpec((rows, d), lambda i, pos: (i, 0))
    vec_spec = pl.BlockSpec((1, d), lambda i, pos: (0, 0))
    return pl.pallas_call(
        functools.partial(_combine_ln_kernel, rows=rows),
        grid_spec=pltpu.PrefetchScalarGridSpec(
            num_scalar_prefetch=1,
            grid=(t // rows,),
            in_specs=[pl.BlockSpec(memory_space=pl.ANY), row_spec, vec_spec, vec_spec],
            out_specs=[row_spec, row_spec],
            scratch_shapes=[pltpu.VMEM((TOP_K, rows, d), F32), pltpu.SemaphoreType.DMA(())]),
        out_shape=[jax.ShapeDtypeStruct((t, d), F32), jax.ShapeDtypeStruct((t, d), BF16)],
        compiler_params=_cparams(("arbitrary",)),
        name="moe_combine_ln",
    )(pos, ys, h, g.reshape(1, d), b.reshape(1, d))


def moe_routing(logits):
    n_tok = logits.shape[0]
    top_val, top_idx = lax.top_k(logits, TOP_K)
    gates = jax.nn.softmax(top_val, axis=-1)
    flat_e = top_idx.reshape(-1)
    flat_g = gates.reshape(-1)
    n_assign = n_tok * TOP_K
    order = jnp.argsort(flat_e, stable=True)
    se = flat_e[order]
    counts = jnp.bincount(flat_e, length=N_EXPERTS)
    padded = (counts + MOE_BLOCK - 1) // MOE_BLOCK * MOE_BLOCK
    start = jnp.cumsum(counts) - counts
    pend = jnp.cumsum(padded)
    pstart = pend - padded
    dest = (pstart[se] + jnp.arange(n_assign, dtype=jnp.int32) - start[se]).astype(jnp.int32)
    n_blocks = -(-n_assign // MOE_BLOCK) + N_EXPERTS
    n_rows = n_blocks * MOE_BLOCK
    row_tok = jnp.zeros((n_rows,), jnp.int32).at[dest].set((order // TOP_K).astype(jnp.int32))
    row_gate = jnp.zeros((n_rows,), F32).at[dest].set(flat_g[order])
    pos = jnp.zeros((n_assign,), jnp.int32).at[order].set(dest)
    block_expert = jnp.minimum(
        jnp.searchsorted(pend, jnp.arange(n_blocks, dtype=pend.dtype) * MOE_BLOCK, side='right'),
        N_EXPERTS - 1).astype(jnp.int32)
    return row_tok, row_gate.reshape(n_rows, 1), pos, block_expert


def _pack_w_in(w):
    d = w.shape[0]
    pad = jnp.zeros((d, LANES - (IKW_END - IKW_START)), w.dtype)
    return jnp.concatenate([w[:, :IKW_START], w[:, IKW_END:], w[:, IKW_START:IKW_END], pad], axis=1).astype(BF16)


def kernel(x, ln_in_g, ln_in_b, w_in, w_out, hgrn_lb, hgrn_norm_g, ret_norm_g, ln1_g, ln1_b,
           router_w, router_b, w_gate_up, b_gate_up, w_down, b_down, ln2_g, ln2_b):
    bsz, seq, d = x.shape
    assert bsz == 1
    t = seq
    lb_all = jnp.cumsum(jax.nn.softmax(hgrn_lb.astype(F32), axis=0), axis=0)
    lb_all = lb_all - lb_all[0:1]
    ang = jnp.arange(t, dtype=F32)[:, None] * (
        1.0 / (RET_THETA ** (jnp.arange(0, C_KEY, 2, dtype=F32) / C_KEY)))[None, :]
    cos_r = jnp.concatenate([jnp.cos(ang), jnp.cos(ang)], axis=1)
    sin_r = jnp.concatenate([-jnp.sin(ang), jnp.sin(ang)], axis=1)

    h, hb = ln_in(x.reshape(t, d), ln_in_g, ln_in_b)
    for l in range(DEPTH):
        p = matmul(hb, _pack_w_in(w_in[l]), tm=512, tn=9 * LANES)
        oa = hgrn_mixer(p, lb_all[l].reshape(1, A_WIDTH), hgrn_norm_g[l].reshape(1, A_WIDTH))
        ob = dsa_mixer(p)
        oc = retention_mixer(p, cos_r, sin_r, ret_norm_g[l].reshape(1, C_WIDTH))
        mix = jnp.concatenate([oa, ob, oc], axis=1)
        mm = matmul(mix, w_out[l].astype(BF16), tm=512, tn=1024)
        rw = jnp.pad(router_w[l], ((0, 0), (0, LANES - N_EXPERTS))).astype(BF16)
        rb = jnp.pad(router_b[l], (0, LANES - N_EXPERTS)).reshape(1, LANES)
        h1, logits = ln1_router(h, mm, ln1_g[l], ln1_b[l], rw, rb)
        row_tok, row_gate, pos, block_expert = moe_routing(logits[:, :N_EXPERTS])
        xs = gather_rows(h1, row_tok)
        ys = expert_ffn(xs, block_expert, row_gate,
                        w_gate_up[l].astype(BF16), b_gate_up[l].reshape(N_EXPERTS, 1, 2 * D_EXPERT),
                        w_down[l].astype(BF16), b_down[l].reshape(N_EXPERTS, 1, d))
        h, hb = combine_ln(ys, pos, h1, ln2_g[l], ln2_b[l])
    return h.reshape(bsz, t, d)
```

```python
import functools
import math

import numpy as np
import jax
import jax.numpy as jnp
from jax import lax
from jax.experimental import pallas as pl
from jax.experimental.pallas import tpu as pltpu

F32 = jnp.float32
BF16 = jnp.bfloat16
HI = lax.Precision.HIGHEST

LANES = 128
VMEM_LIMIT = 56 * 1024 * 1024

D_MODEL = 4096
DEPTH = 2
CHUNK = 64

A_KEY = 128
A_VAL = 128
A_WIDTH = 1536
A_HEADS = 12

B_WIDTH = 1024
B_HEAD_DIM = 128
B_HEADS = 8
B_ROT = 32
IDX_HEADS = 16
IDX_DIM = 64
IDX_ROT = 16
IDX_TOPK_MAX = 256
ROPE_THETA = 500000.0

C_WIDTH = 1536
C_VAL = 256
C_KEY = 128
C_HEADS = 6
RET_THETA = 10000.0

N_EXPERTS = 32
TOP_K = 4
D_EXPERT = 768
SWIGLU_LIMIT = 7.0
SWIGLU_ALPHA = 1.702
MOE_BLOCK = 256

DN_ALPHA = (2 * DEPTH) ** 0.25
LN_EPS = 1e-5
NORM_EPS = 1e-6

OFF_AQ, OFF_AF, OFF_AI, OFF_AG = 0, 12, 24, 36
OFF_BQ, OFF_BK, OFF_BV, OFF_IQ = 48, 56, 64, 72
OFF_CQ, OFF_CK, OFF_CV, OFF_CG, OFF_IKW = 80, 86, 92, 104, 116
PROJ_BLOCKS = 117
PROJ_PAD = PROJ_BLOCKS * LANES
IKW_START, IKW_END = 10240, 10320

INT_MIN = -2 ** 31
COUNT_ROWS = 128


def _cparams(sem, vmem=VMEM_LIMIT):
    return pltpu.CompilerParams(dimension_semantics=sem, vmem_limit_bytes=vmem)


def _ln_rows(x, g, b):
    mu = jnp.mean(x, axis=-1, keepdims=True)
    xc = x - mu
    var = jnp.mean(xc * xc, axis=-1, keepdims=True)
    return xc * lax.rsqrt(var + LN_EPS) * g + b


def _ln_in_kernel(x_ref, g_ref, b_ref, o_ref, ob_ref):
    y = _ln_rows(x_ref[...], g_ref[...], b_ref[...])
    o_ref[...] = y
    ob_ref[...] = y.astype(BF16)


def ln_in(x, g, b, rows=256):
    t, d = x.shape
    rows = min(rows, t)
    row_spec = pl.BlockSpec((rows, d), lambda i: (i, 0))
    vec_spec = pl.BlockSpec((1, d), lambda i: (0, 0))
    return pl.pallas_call(
        _ln_in_kernel,
        grid=(t // rows,),
        in_specs=[row_spec, vec_spec, vec_spec],
        out_specs=[row_spec, row_spec],
        out_shape=[jax.ShapeDtypeStruct((t, d), F32), jax.ShapeDtypeStruct((t, d), BF16)],
        compiler_params=_cparams(("parallel",)),
        name="ln_in",
    )(x, g.reshape(1, d), b.reshape(1, d))


def _ln1_router_kernel(h_ref, mm_ref, g_ref, b_ref, rw_ref, rb_ref, o_ref, lg_ref):
    y = _ln_rows(DN_ALPHA * h_ref[...] + mm_ref[...], g_ref[...], b_ref[...])
    o_ref[...] = y
    lg_ref[...] = jnp.dot(y.astype(BF16), rw_ref[...], preferred_element_type=F32) + rb_ref[...]


def ln1_router(h, mm, g, b, rw, rb, rows=256):
    t, d = h.shape
    rows = min(rows, t)
    row_spec = pl.BlockSpec((rows, d), lambda i: (i, 0))
    vec_spec = pl.BlockSpec((1, d), lambda i: (0, 0))
    return pl.pallas_call(
        _ln1_router_kernel,
        grid=(t // rows,),
        in_specs=[row_spec, row_spec, vec_spec, vec_spec,
                  pl.BlockSpec((d, LANES), lambda i: (0, 0)),
                  pl.BlockSpec((1, LANES), lambda i: (0, 0))],
        out_specs=[row_spec, pl.BlockSpec((rows, LANES), lambda i: (i, 0))],
        out_shape=[jax.ShapeDtypeStruct((t, d), F32), jax.ShapeDtypeStruct((t, LANES), F32)],
        compiler_params=_cparams(("parallel",)),
        name="ln1_router",
    )(h, mm, g.reshape(1, d), b.reshape(1, d), rw, rb)


def _matmul_kernel(a_ref, b_ref, o_ref):
    o_ref[...] = jnp.dot(a_ref[...], b_ref[...], preferred_element_type=F32)


def matmul(a, b, tm, tn):
    m, k = a.shape
    n = b.shape[1]
    tm = min(tm, m)
    return pl.pallas_call(
        _matmul_kernel,
        grid=(n // tn, m // tm),
        in_specs=[pl.BlockSpec((tm, k), lambda j, i: (i, 0)),
                  pl.BlockSpec((k, tn), lambda j, i: (0, j))],
        out_specs=pl.BlockSpec((tm, tn), lambda j, i: (i, j)),
        out_shape=jax.ShapeDtypeStruct((m, n), F32),
        compiler_params=_cparams(("parallel", "parallel")),
        name="matmul",
    )(a, b)


def _hgrn_kernel(q_ref, f_ref, i_ref, g_ref, lb_ref, ng_ref, o_ref,
                 state_ref, cum_ref, kk_ref, acc_ref, *, n_chunks):
    @pl.when(pl.program_id(1) == 0)
    def _():
        state_ref[...] = jnp.zeros_like(state_ref)

    lb = lb_ref[...]
    ng = ng_ref[...]
    row = lax.broadcasted_iota(jnp.int32, (CHUNK, CHUNK), 0)
    col = lax.broadcasted_iota(jnp.int32, (CHUNK, CHUNK), 1)
    tri = (col <= row).astype(F32)
    sub = 8

    for c in range(n_chunks):
        r0 = c * CHUNK
        f = lb + (1.0 - lb) * jax.nn.sigmoid(f_ref[r0:r0 + CHUNK, :])
        kk = 1.0 - f
        cum = jnp.dot(tri, jnp.log(f), precision=HI, preferred_element_type=F32)
        qs = q_ref[r0:r0 + CHUNK, :] * (A_KEY ** -0.5)
        inter = lax.dot_general(qs * jnp.exp(cum), state_ref[...], (((1,), (1,)), ((), ())),
                                precision=HI, preferred_element_type=F32)
        cum_ref[...] = cum
        kk_ref[...] = kk
        acc_ref[...] = inter

        for sb in range(CHUNK // sub):
            t0 = sb * sub
            qs_t = qs[t0:, :]
            cum_t = cum[t0:, :]
            t_idx = t0 + lax.broadcasted_iota(jnp.int32, (CHUNK - t0, 1), 0)

            def body(s, acc_t, qs_t=qs_t, cum_t=cum_t, t_idx=t_idx, r0=r0):
                rc = cum_ref[pl.ds(s, 1), :]
                rk = kk_ref[pl.ds(s, 1), :]
                rv = i_ref[pl.ds(r0 + s, 1), :]
                dec = jnp.exp(jnp.minimum(cum_t - rc, 0.0))
                a = jnp.sum(qs_t * rk * dec, axis=1, keepdims=True)
                a = jnp.where(t_idx >= s, a, 0.0)
                return acc_t + a * rv

            acc_t = lax.fori_loop(t0, t0 + sub, body, jnp.zeros((CHUNK - t0, A_VAL), F32), unroll=True)
            acc_ref[t0:, :] += acc_t

        last = cum[CHUNK - 1:CHUNK, :]
        kd = kk * jnp.exp(last - cum)
        v = i_ref[r0:r0 + CHUNK, :]
        state_ref[...] = state_ref[...] * jnp.exp(last) + lax.dot_general(
            v, kd, (((0,), (0,)), ((), ())), precision=HI, preferred_element_type=F32)

        o = acc_ref[...]
        o = o * lax.rsqrt(jnp.mean(o * o, axis=-1, keepdims=True) + NORM_EPS) * ng
        g = g_ref[r0:r0 + CHUNK, :]
        o_ref[r0:r0 + CHUNK, :] = (o * (g * jax.nn.sigmoid(g))).astype(o_ref.dtype)


def hgrn_mixer(p, lb, ng, rows=256):
    t = p.shape[0]
    rows = min(rows, t)

    def seg(off):
        return pl.BlockSpec((rows, LANES), lambda h, r: (r, off + h))

    vec = pl.BlockSpec((1, LANES), lambda h, r: (0, h))
    return pl.pallas_call(
        functools.partial(_hgrn_kernel, n_chunks=rows // CHUNK),
        grid=(A_HEADS, t // rows),
        in_specs=[seg(OFF_AQ), seg(OFF_AF), seg(OFF_AI), seg(OFF_AG), vec, vec],
        out_specs=pl.BlockSpec((rows, LANES), lambda h, r: (r, h)),
        out_shape=jax.ShapeDtypeStruct((t, A_WIDTH), BF16),
        scratch_shapes=[pltpu.VMEM((A_VAL, A_KEY), F32), pltpu.VMEM((CHUNK, A_KEY), F32),
                        pltpu.VMEM((CHUNK, A_KEY), F32), pltpu.VMEM((CHUNK, A_VAL), F32)],
        compiler_params=_cparams(("parallel", "arbitrary")),
        name="hgrn_mixer",
    )(p, p, p, p, lb, ng)


def _ret_consts():
    lg = np.log(1.0 - np.exp(np.linspace(np.log(1.0 / 32), np.log(1.0 / 512), C_HEADS)))
    pos = np.arange(CHUNK, dtype=np.float64)
    rel = pos[:, None] - pos[None, :]
    dmat = np.where(rel >= 0, np.exp(lg[:, None, None] * np.maximum(rel, 0.0)), 0.0)
    q_dec = np.broadcast_to(np.exp(lg[:, None] * (pos + 1.0))[..., None], (C_HEADS, CHUNK, C_KEY))
    k_dec = np.broadcast_to(np.exp(lg[:, None] * (CHUNK - 1.0 - pos))[..., None], (C_HEADS, CHUNK, C_KEY))
    c_dec = np.broadcast_to(np.exp(lg * CHUNK)[:, None, None], (C_HEADS, 1, C_VAL))
    return tuple(jnp.asarray(np.ascontiguousarray(a), F32) for a in (dmat, q_dec, k_dec, c_dec))


def _ret_kernel(q_ref, k_ref, v_ref, g_ref, cos_ref, sin_ref, dm_ref, qd_ref, kd_ref, cd_ref, ng_ref,
                o_ref, state_ref, *, n_chunks):
    @pl.when(pl.program_id(1) == 0)
    def _():
        state_ref[...] = jnp.zeros_like(state_ref)

    dmat = dm_ref[...]
    q_dec = qd_ref[...]
    k_dec = kd_ref[...]
    c_dec = cd_ref[...]
    ng = ng_ref[...]
    for c in range(n_chunks):
        r0 = c * CHUNK
        cos = cos_ref[r0:r0 + CHUNK, :]
        sin = sin_ref[r0:r0 + CHUNK, :]
        q = q_ref[r0:r0 + CHUNK, :]
        k = k_ref[r0:r0 + CHUNK, :]
        q = q * cos + pltpu.roll(q, C_KEY // 2, 1) * sin
        k = (k * cos + pltpu.roll(k, C_KEY // 2, 1) * sin) * (C_KEY ** -0.5)
        v = v_ref[r0:r0 + CHUNK, :]
        s = lax.dot_general(q, k, (((1,), (1,)), ((), ())), precision=HI, preferred_element_type=F32)
        intra = jnp.dot(s * dmat, v, precision=HI, preferred_element_type=F32)
        inter = jnp.dot(q * q_dec, state_ref[...], precision=HI, preferred_element_type=F32)
        state_ref[...] = c_dec * state_ref[...] + lax.dot_general(
            k * k_dec, v, (((0,), (0,)), ((), ())), precision=HI, preferred_element_type=F32)
        o = intra + inter
        mu = jnp.mean(o, axis=-1, keepdims=True)
        oc = o - mu
        var = jnp.mean(oc * oc, axis=-1, keepdims=True)
        o = oc * lax.rsqrt(var + NORM_EPS) * ng
        g = g_ref[r0:r0 + CHUNK, :]
        o_ref[r0:r0 + CHUNK, :] = (o * (g * jax.nn.sigmoid(g))).astype(o_ref.dtype)


def retention_mixer(p, cos, sin, ng, rows=256):
    t = p.shape[0]
    rows = min(rows, t)
    dmat, q_dec, k_dec, c_dec = _ret_consts()
    tab = pl.BlockSpec((rows, C_KEY), lambda h, r: (r, 0))
    return pl.pallas_call(
        functools.partial(_ret_kernel, n_chunks=rows // CHUNK),
        grid=(C_HEADS, t // rows),
        in_specs=[pl.BlockSpec((rows, C_KEY), lambda h, r: (r, OFF_CQ + h)),
                  pl.BlockSpec((rows, C_KEY), lambda h, r: (r, OFF_CK + h)),
                  pl.BlockSpec((rows, C_VAL), lambda h, r: (r, OFF_CV // 2 + h)),
                  pl.BlockSpec((rows, C_VAL), lambda h, r: (r, OFF_CG // 2 + h)),
                  tab, tab,
                  pl.BlockSpec((None, CHUNK, CHUNK), lambda h, r: (h, 0, 0)),
                  pl.BlockSpec((None, CHUNK, C_KEY), lambda h, r: (h, 0, 0)),
                  pl.BlockSpec((None, CHUNK, C_KEY), lambda h, r: (h, 0, 0)),
                  pl.BlockSpec((None, 1, C_VAL), lambda h, r: (h, 0, 0)),
                  pl.BlockSpec((1, C_VAL), lambda h, r: (0, h))],
        out_specs=pl.BlockSpec((rows, C_VAL), lambda h, r: (r, h)),
        out_shape=jax.ShapeDtypeStruct((t, C_WIDTH), BF16),
        scratch_shapes=[pltpu.VMEM((C_KEY, C_VAL), F32)],
        compiler_params=_cparams(("parallel", "arbitrary")),
        name="retention_mixer",
    )(p, p, p, p, cos, sin, dmat, q_dec, k_dec, c_dec, ng)


def _rope_tables(t, n_rot, theta, width):
    half = n_rot // 2
    inv_freq = 1.0 / (theta ** (jnp.arange(0, n_rot, 2, dtype=F32) / n_rot))
    ang = jnp.arange(t, dtype=F32)[:, None] * inv_freq[None, :]
    c, s = jnp.cos(ang), jnp.sin(ang)
    z = jnp.zeros((t, width - n_rot), F32)
    zh = jnp.zeros((t, half), F32)
    cos = jnp.concatenate([c, c, jnp.ones((t, width - n_rot), F32)], axis=1)
    sin_lo = jnp.concatenate([-s, zh, z], axis=1)
    sin_hi = jnp.concatenate([zh, s, z], axis=1)
    rep = LANES // width
    return tuple(jnp.tile(a, (1, rep)) for a in (cos, sin_lo, sin_hi))


def _rope128(x, cos, sin_lo, sin_hi, half):
    return x * cos + pltpu.roll(x, LANES - half, 1) * sin_lo + pltpu.roll(x, half, 1) * sin_hi


def _dsa_prep_kernel(q_ref, k_ref, v_ref, iq_ref, ikw_ref, cb_ref, slb_ref, shb_ref, ci_ref, sli_ref, shi_ref,
                     qo_ref, ko_ref, vo_ref, iqo_ref, iko_ref, wo_ref):
    cb, slb, shb = cb_ref[...], slb_ref[...], shb_ref[...]
    ci, sli, shi = ci_ref[...], sli_ref[...], shi_ref[...]
    for h in range(B_HEADS):
        sl = slice(h * LANES, (h + 1) * LANES)
        qo_ref[:, sl] = _rope128(q_ref[:, sl], cb, slb, shb, B_ROT // 2).astype(BF16)
        ko_ref[:, sl] = _rope128(k_ref[:, sl], cb, slb, shb, B_ROT // 2).astype(BF16)
        iqo_ref[:, sl] = _rope128(iq_ref[:, sl], ci, sli, shi, IDX_ROT // 2).astype(BF16)
    vo_ref[...] = v_ref[...].astype(BF16)
    ikw = ikw_ref[...]
    iko_ref[...] = _rope128(ikw, ci, sli, shi, IDX_ROT // 2)[:, :IDX_DIM].astype(BF16)
    wo_ref[...] = ikw[:, IDX_DIM:IDX_DIM + IDX_HEADS] * (IDX_DIM ** -0.5 * IDX_HEADS ** -0.5)


def dsa_prep(p, rows=256):
    t = p.shape[0]
    rows = min(rows, t)
    tabs_b = _rope_tables(t, B_ROT, ROPE_THETA, B_HEAD_DIM)
    tabs_i = _rope_tables(t, IDX_ROT, ROPE_THETA, IDX_DIM)
    wide = lambda off: pl.BlockSpec((rows, B_WIDTH), lambda r: (r, off // 8))
    tab = pl.BlockSpec((rows, LANES), lambda r: (r, 0))
    out_wide = pl.BlockSpec((rows, B_WIDTH), lambda r: (r, 0))
    return pl.pallas_call(
        _dsa_prep_kernel,
        grid=(t // rows,),
        in_specs=[wide(OFF_BQ), wide(OFF_BK), wide(OFF_BV), wide(OFF_IQ),
                  pl.BlockSpec((rows, LANES), lambda r: (r, OFF_IKW))] + [tab] * 6,
        out_specs=[out_wide, out_wide, out_wide, out_wide,
                   pl.BlockSpec((rows, IDX_DIM), lambda r: (r, 0)),
                   pl.BlockSpec((rows, IDX_HEADS), lambda r: (r, 0))],
        out_shape=[jax.ShapeDtypeStruct((t, B_WIDTH), BF16)] * 4
        + [jax.ShapeDtypeStruct((t, IDX_DIM), BF16), jax.ShapeDtypeStruct((t, IDX_HEADS), F32)],
        compiler_params=_cparams(("parallel",)),
        name="dsa_prep",
    )(p, p, p, p, p, *tabs_b, *tabs_i)


def _dsa_kernel(q_ref, qi_ref, w_ref, kit_ref, k_ref, v_ref, o_ref,
                keys_ref, thr_ref, tie_ref, m_ref, l_ref, acc_ref, *, qb, kb, topk, pos_bits):
    i = pl.program_id(0)
    j = pl.program_id(1)
    nkb = ((i + 1) * qb + kb - 1) // kb

    @pl.when(j == 0)
    def _():
        q_pos = i * qb + lax.broadcasted_iota(jnp.int32, (qb, 1), 0)
        limit = (q_pos // CHUNK + 1) * CHUNK
        w = w_ref[...]

        def score_block(b, carry):
            off = pl.multiple_of(b * kb, kb)
            kit = kit_ref[:, pl.ds(off, kb)]
            sc = jnp.zeros((qb, kb), F32)
            for h in range(IDX_HEADS):
                lg = jnp.dot(qi_ref[:, h * IDX_DIM:(h + 1) * IDX_DIM], kit, preferred_element_type=F32)
                sc = sc + jnp.maximum(lg, 0.0) * w[:, h:h + 1]
            bits = pltpu.bitcast(sc, jnp.int32)
            key = jnp.where(bits < 0, bits ^ jnp.int32(0x7FFFFFFF), bits)
            k_pos = off + lax.broadcasted_iota(jnp.int32, (1, kb), 1)
            keys_ref[:, pl.ds(off, kb)] = jnp.where(k_pos < limit, key, jnp.int32(INT_MIN))
            return carry

        lax.fori_loop(0, nkb, score_block, 0)

        def count_rows(make_hit):
            cnts = []
            for g in range(qb // COUNT_ROWS):
                rows = slice(g * COUNT_ROWS, (g + 1) * COUNT_ROWS)
                hit = make_hit(rows)

                def count_block(b, cnt, rows=rows, hit=hit):
                    off = pl.multiple_of(b * kb, kb)
                    for c in range(kb // LANES):
                        blk = keys_ref[rows, pl.ds(off + c * LANES, LANES)]
                        cnt = cnt + hit(blk, off + c * LANES).astype(jnp.int32)
                    return cnt

                cnts.append(lax.fori_loop(0, nkb, count_block, jnp.zeros((COUNT_ROWS, LANES), jnp.int32)))
            return jnp.sum(jnp.concatenate(cnts, axis=0), axis=1, keepdims=True)

        def count_ge(cand):
            cand_b = jnp.broadcast_to(cand, (qb, LANES))
            return count_rows(lambda rows: (lambda blk, pos, c=cand_b[rows]: blk >= c))

        def bit_step(it, thr):
            cand = thr ^ lax.shift_left(jnp.int32(1), 31 - it)
            return jnp.where(count_ge(cand) >= topk, cand, thr)

        thr = lax.fori_loop(0, 32, bit_step, jnp.full((qb, 1), INT_MIN, jnp.int32))
        thr = jnp.maximum(thr, jnp.int32(INT_MIN + 1))
        thr_ref[...] = thr

        n_ge = count_ge(thr)
        tie_ref[...] = jnp.full_like(tie_ref, 2 ** 31 - 1)

        @pl.when(jnp.max(n_ge) > topk)
        def _():
            need = topk - count_ge(thr + 1)
            thr_b = jnp.broadcast_to(thr, (qb, LANES))
            lane = lax.broadcasted_iota(jnp.int32, (COUNT_ROWS, LANES), 1)

            def pos_step(it, x):
                cand = x | lax.shift_left(jnp.int32(1), pos_bits - 1 - it)
                cand_b = jnp.broadcast_to(cand, (qb, LANES))
                tied_before = count_rows(
                    lambda rows: (lambda blk, pos, th=thr_b[rows], c=cand_b[rows]: (blk == th) & (lane < c - pos)))
                return jnp.where(tied_before < need, cand, x)

            x = lax.fori_loop(0, pos_bits, pos_step, jnp.zeros((qb, 1), jnp.int32))
            tie_ref[...] = jnp.where(n_ge > topk, x, 2 ** 31 - 1)

        m_ref[...] = jnp.full_like(m_ref, -1e30)
        l_ref[...] = jnp.zeros_like(l_ref)
        acc_ref[...] = jnp.zeros_like(acc_ref)

    @pl.when(j < nkb)
    def _():
        off = pl.multiple_of(j * kb, kb)
        key = keys_ref[:, pl.ds(off, kb)]
        thr = thr_ref[...]
        k_pos = off + lax.broadcasted_iota(jnp.int32, (1, kb), 1)
        mask = (key >= thr) & ((key > thr) | (k_pos <= tie_ref[...]))
        for h in range(B_HEADS):
            sl = slice(h * B_HEAD_DIM, (h + 1) * B_HEAD_DIM)
            s = lax.dot_general(q_ref[:, sl], k_ref[:, sl], (((1,), (1,)), ((), ())),
                                preferred_element_type=F32) * (B_HEAD_DIM ** -0.5)
            s = jnp.where(mask, s, -1e30)
            m_old = m_ref[h]
            m_new = jnp.maximum(m_old, jnp.max(s, axis=1, keepdims=True))
            alpha = jnp.exp(m_old - m_new)
            p = jnp.exp(s - m_new)
            l_ref[h] = alpha * l_ref[h] + jnp.sum(p, axis=1, keepdims=True)
            acc_ref[:, sl] = alpha * acc_ref[:, sl] + jnp.dot(p.astype(BF16), v_ref[:, sl],
                                                              preferred_element_type=F32)
            m_ref[h] = m_new

    @pl.when(j == nkb - 1)
    def _():
        for h in range(B_HEADS):
            sl = slice(h * B_HEAD_DIM, (h + 1) * B_HEAD_DIM)
            o_ref[:, sl] = (acc_ref[:, sl] / l_ref[h]).astype(o_ref.dtype)


def dsa_mixer(p, qb=256, kb=1024):
    t = p.shape[0]
    qb = min(qb, t)
    kb = min(kb, t)
    topk = min(IDX_TOPK_MAX, t // 4)
    q, k, v, qi, ki, wi = dsa_prep(p)
    kit = ki.T

    def kv_map(i, j):
        return (jnp.minimum(j, ((i + 1) * qb + kb - 1) // kb - 1), 0)

    q_spec = pl.BlockSpec((qb, B_WIDTH), lambda i, j: (i, 0))
    return pl.pallas_call(
        functools.partial(_dsa_kernel, qb=qb, kb=kb, topk=topk, pos_bits=max(1, (t - 1).bit_length())),
        grid=(t // qb, t // kb),
        in_specs=[q_spec, q_spec,
                  pl.BlockSpec((qb, IDX_HEADS), lambda i, j: (i, 0)),
                  pl.BlockSpec((IDX_DIM, t), lambda i, j: (0, 0)),
                  pl.BlockSpec((kb, B_WIDTH), kv_map),
                  pl.BlockSpec((kb, B_WIDTH), kv_map)],
        out_specs=q_spec,
        out_shape=jax.ShapeDtypeStruct((t, B_WIDTH), BF16),
        scratch_shapes=[pltpu.VMEM((qb, t), jnp.int32), pltpu.VMEM((qb, 1), jnp.int32),
                        pltpu.VMEM((qb, 1), jnp.int32),
                        pltpu.VMEM((B_HEADS, qb, 1), F32), pltpu.VMEM((B_HEADS, qb, 1), F32),
                        pltpu.VMEM((qb, B_WIDTH), F32)],
        compiler_params=_cparams(("arbitrary", "arbitrary")),
        name="dsa_mixer",
    )(q, qi, wi, kit, k, v)


def _gather_rows_kernel(tok_ref, x_hbm, o_hbm, sem, *, rows):
    b = pl.program_id(0)
    last = pl.num_programs(0) - 1

    def row_copy(blk, r, src_row):
        return pltpu.make_async_copy(x_hbm.at[pl.ds(src_row, 1), :], o_hbm.at[pl.ds(blk * rows + r, 1), :],
                                     sem.at[blk % 2])

    def issue_block(blk):
        def issue(r, carry):
            row_copy(blk, r, tok_ref[blk * rows + r]).start()
            return carry
        lax.fori_loop(0, rows, issue, 0)

    def drain_block(blk):
        def drain(r, carry):
            row_copy(blk, r, 0).wait()
            return carry
        lax.fori_loop(0, rows, drain, 0)

    issue_block(b)

    @pl.when(b >= 1)
    def _():
        drain_block(b - 1)

    @pl.when(b == last)
    def _():
        drain_block(b)


def gather_rows(x, row_tok, rows=MOE_BLOCK):
    n_rows = row_tok.shape[0]
    d = x.shape[1]
    return pl.pallas_call(
        functools.partial(_gather_rows_kernel, rows=rows),
        grid_spec=pltpu.PrefetchScalarGridSpec(
            num_scalar_prefetch=1,
            grid=(n_rows // rows,),
            in_specs=[pl.BlockSpec(memory_space=pl.ANY)],
            out_specs=pl.BlockSpec(memory_space=pl.ANY),
            scratch_shapes=[pltpu.SemaphoreType.DMA((2,))]),
        out_shape=jax.ShapeDtypeStruct((n_rows, d), x.dtype),
        compiler_params=_cparams(("arbitrary",)),
        name="moe_gather",
    )(row_tok, x)


def _expert_kernel(be_ref, nu_ref, x_ref, wg_ref, wu_ref, bg_ref, bu_ref, wd_ref, bd_ref, gate_ref, o_ref):
    f = pl.program_id(1)
    used = pl.program_id(0) < nu_ref[0]

    @pl.when(jnp.logical_not(used) & (f == 0))
    def _():
        o_ref[...] = jnp.zeros_like(o_ref)

    @pl.when(used)
    def _():
        x = x_ref[...].astype(BF16)
        g_lin = jnp.dot(x, wg_ref[...], preferred_element_type=F32) + bg_ref[...]
        u_lin = jnp.dot(x, wu_ref[...], preferred_element_type=F32) + bu_ref[...]
        g_lin = jnp.minimum(g_lin, SWIGLU_LIMIT)
        u_lin = jnp.clip(u_lin, -SWIGLU_LIMIT, SWIGLU_LIMIT)
        act = (u_lin + 1.0) * g_lin * jax.nn.sigmoid(SWIGLU_ALPHA * g_lin)
        part = jnp.dot(act.astype(BF16), wd_ref[...], preferred_element_type=F32)

        @pl.when(f == 0)
        def _():
            o_ref[...] = part

        @pl.when(f == pl.num_programs(1) - 1)
        def _():
            o_ref[...] = (o_ref[...] + part + bd_ref[...]) * gate_ref[...]


def expert_ffn(xs, block_expert, n_used, row_gate, w_gu, b_gu, w_d, b_d):
    n_rows, d = xs.shape
    f_tiles = 2
    ft = D_EXPERT // f_tiles

    def blk(b, nu):
        return jnp.minimum(b, nu[0] - 1)

    def tile(b, f, nu):
        f = jnp.where(b < nu[0], f, f_tiles - 1)
        return jnp.where(blk(b, nu) % 2 == 0, f, f_tiles - 1 - f)

    return pl.pallas_call(
        _expert_kernel,
        grid_spec=pltpu.PrefetchScalarGridSpec(
            num_scalar_prefetch=2,
            grid=(n_rows // MOE_BLOCK, f_tiles),
            in_specs=[pl.BlockSpec((MOE_BLOCK, d), lambda b, f, be, nu: (blk(b, nu), 0)),
                      pl.BlockSpec((None, d, ft), lambda b, f, be, nu: (be[blk(b, nu)], 0, tile(b, f, nu))),
                      pl.BlockSpec((None, d, ft), lambda b, f, be, nu: (be[blk(b, nu)], 0, f_tiles + tile(b, f, nu))),
                      pl.BlockSpec((None, 1, ft), lambda b, f, be, nu: (be[blk(b, nu)], 0, tile(b, f, nu))),
                      pl.BlockSpec((None, 1, ft), lambda b, f, be, nu: (be[blk(b, nu)], 0, f_tiles + tile(b, f, nu))),
                      pl.BlockSpec((None, ft, d), lambda b, f, be, nu: (be[blk(b, nu)], tile(b, f, nu), 0)),
                      pl.BlockSpec((None, 1, d), lambda b, f, be, nu: (be[blk(b, nu)], 0, 0)),
                      pl.BlockSpec((MOE_BLOCK, 1), lambda b, f, be, nu: (blk(b, nu), 0))],
            out_specs=pl.BlockSpec((MOE_BLOCK, d), lambda b, f, be, nu: (b, 0))),
        out_shape=jax.ShapeDtypeStruct((n_rows, d), F32),
        compiler_params=_cparams(("arbitrary", "arbitrary")),
        name="moe_experts",
    )(block_expert, n_used, xs, w_gu, w_gu, b_gu, b_gu, w_d, b_d, row_gate)


def _combine_ln_kernel(pos_ref, ys_hbm, h_ref, g_ref, b_ref, o_ref, ob_ref, buf_ref, sem, *, rows):
    i = pl.program_id(0)

    def row_copy(tile, r, k, src_row):
        slot = tile % 2
        return pltpu.make_async_copy(ys_hbm.at[pl.ds(src_row, 1), :], buf_ref.at[slot, k, pl.ds(r, 1), :],
                                     sem.at[slot])

    def issue_tile(tile):
        def issue(r, carry):
            for k in range(TOP_K):
                row_copy(tile, r, k, pos_ref[(tile * rows + r) * TOP_K + k]).start()
            return carry
        lax.fori_loop(0, rows, issue, 0)

    @pl.when(i == 0)
    def _():
        issue_tile(0)

    @pl.when(i + 1 < pl.num_programs(0))
    def _():
        issue_tile(i + 1)

    def drain(r, carry):
        for k in range(TOP_K):
            row_copy(i, r, k, 0).wait()
        return carry

    lax.fori_loop(0, rows, drain, 0)
    slot = i % 2
    ffn = (buf_ref[slot, 0] + buf_ref[slot, 1]) + (buf_ref[slot, 2] + buf_ref[slot, 3])
    y = _ln_rows(DN_ALPHA * h_ref[...] + ffn, g_ref[...], b_ref[...])
    o_ref[...] = y
    ob_ref[...] = y.astype(BF16)


def combine_ln(ys, pos, h, g, b, rows=64):
    t, d = h.shape
    rows = min(rows, t)
    row_spec = pl.BlockSpec((rows, d), lambda i, pos: (i, 0))
    vec_spec = pl.BlockSpec((1, d), lambda i, pos: (0, 0))
    return pl.pallas_call(
        functools.partial(_combine_ln_kernel, rows=rows),
        grid_spec=pltpu.PrefetchScalarGridSpec(
            num_scalar_prefetch=1,
            grid=(t // rows,),
            in_specs=[pl.BlockSpec(memory_space=pl.ANY), row_spec, vec_spec, vec_spec],
            out_specs=[row_spec, row_spec],
            scratch_shapes=[pltpu.VMEM((2, TOP_K, rows, d), F32), pltpu.SemaphoreType.DMA((2,))]),
        out_shape=[jax.ShapeDtypeStruct((t, d), F32), jax.ShapeDtypeStruct((t, d), BF16)],
        compiler_params=_cparams(("arbitrary",)),
        name="moe_combine_ln",
    )(pos, ys, h, g.reshape(1, d), b.reshape(1, d))


def moe_routing(logits):
    n_tok = logits.shape[0]
    n_assign = n_tok * TOP_K
    n_blocks = -(-n_assign // MOE_BLOCK) + N_EXPERTS
    top_val, top_idx = lax.top_k(logits, TOP_K)
    gates = jax.nn.softmax(top_val, axis=-1)
    flat_e = top_idx.reshape(-1).astype(jnp.int32)
    flat_g = gates.reshape(-1)
    iota = jnp.arange(n_assign, dtype=jnp.int32)
    se, order, sg = lax.sort((flat_e, iota, flat_g), num_keys=1, is_stable=True)
    experts = jnp.arange(N_EXPERTS, dtype=jnp.int32)
    counts = jnp.sum((flat_e[:, None] == experts[None, :]).astype(jnp.int32), axis=0)
    padded = (counts + MOE_BLOCK - 1) // MOE_BLOCK * MOE_BLOCK
    start = jnp.cumsum(counts) - counts
    pend = jnp.cumsum(padded)
    shift = pend - padded - start
    dest = iota + jnp.sum(jnp.where(se[:, None] == experts[None, :], shift[None, :], 0), axis=1)
    _, pos = lax.sort((order, dest), num_keys=1)
    blk = jnp.arange(n_blocks, dtype=jnp.int32)
    block_expert = jnp.minimum(
        jnp.sum((blk[:, None] * MOE_BLOCK >= pend[None, :]).astype(jnp.int32), axis=1), N_EXPERTS - 1)
    first = blk * MOE_BLOCK - shift[block_expert]
    n_valid = jnp.clip(start[block_expert] + counts[block_expert] - first, 0, MOE_BLOCK)
    valid = jnp.arange(MOE_BLOCK, dtype=jnp.int32)[None, :] < n_valid[:, None]

    def runs(a):
        a = jnp.concatenate([a, jnp.zeros((MOE_BLOCK,), a.dtype)])
        take = jax.vmap(lambda s: lax.dynamic_slice(a, (s,), (MOE_BLOCK,)))(first)
        return jnp.where(valid, take, jnp.zeros((), a.dtype)).reshape(-1)

    row_tok = runs(order // TOP_K)
    row_gate = runs(sg).reshape(-1, 1)
    n_used = (pend[-1:] // MOE_BLOCK).astype(jnp.int32)
    return row_tok, row_gate, pos, block_expert, n_used


def _pack_w_in(w):
    d = w.shape[0]
    pad = jnp.zeros((d, LANES - (IKW_END - IKW_START)), w.dtype)
    return jnp.concatenate([w[:, :IKW_START], w[:, IKW_END:], w[:, IKW_START:IKW_END], pad], axis=1).astype(BF16)


def kernel(x, ln_in_g, ln_in_b, w_in, w_out, hgrn_lb, hgrn_norm_g, ret_norm_g, ln1_g, ln1_b,
           router_w, router_b, w_gate_up, b_gate_up, w_down, b_down, ln2_g, ln2_b):
    bsz, seq, d = x.shape
    assert bsz == 1
    t = seq
    lb_all = jnp.cumsum(jax.nn.softmax(hgrn_lb.astype(F32), axis=0), axis=0)
    lb_all = lb_all - lb_all[0:1]
    ang = jnp.arange(t, dtype=F32)[:, None] * (
        1.0 / (RET_THETA ** (jnp.arange(0, C_KEY, 2, dtype=F32) / C_KEY)))[None, :]
    cos_r = jnp.concatenate([jnp.cos(ang), jnp.cos(ang)], axis=1)
    sin_r = jnp.concatenate([-jnp.sin(ang), jnp.sin(ang)], axis=1)

    h, hb = ln_in(x.reshape(t, d), ln_in_g, ln_in_b)
    for l in range(DEPTH):
        p = matmul(hb, _pack_w_in(w_in[l]), tm=512, tn=9 * LANES)
        oa = hgrn_mixer(p, lb_all[l].reshape(1, A_WIDTH), hgrn_norm_g[l].reshape(1, A_WIDTH))
        ob = dsa_mixer(p)
        oc = retention_mixer(p, cos_r, sin_r, ret_norm_g[l].reshape(1, C_WIDTH))
        mix = jnp.concatenate([oa, ob, oc], axis=1)
        mm = matmul(mix, w_out[l].astype(BF16), tm=512, tn=1024)
        rw = jnp.pad(router_w[l], ((0, 0), (0, LANES - N_EXPERTS))).astype(BF16)
        rb = jnp.pad(router_b[l], (0, LANES - N_EXPERTS)).reshape(1, LANES)
        h1, logits = ln1_router(h, mm, ln1_g[l], ln1_b[l], rw, rb)
        row_tok, row_gate, pos, block_expert, n_used = moe_routing(logits[:, :N_EXPERTS])
        xs = gather_rows(h1, row_tok)
        ys = expert_ffn(xs, block_expert, n_used, row_gate,
                        w_gate_up[l].astype(BF16), b_gate_up[l].reshape(N_EXPERTS, 1, 2 * D_EXPERT),
                        w_down[l].astype(BF16), b_down[l].reshape(N_EXPERTS, 1, d))
        h, hb = combine_ln(ys, pos, h1, ln2_g[l], ln2_b[l])
    return h.reshape(bsz, t, d)
```

```python
import functools
import math

import numpy as np
import jax
import jax.numpy as jnp
from jax import lax
from jax.experimental import pallas as pl
from jax.experimental.pallas import tpu as pltpu

F32 = jnp.float32
BF16 = jnp.bfloat16
HI = lax.Precision.HIGHEST

LANES = 128
VMEM_LIMIT = 56 * 1024 * 1024

D_MODEL = 4096
DEPTH = 2
CHUNK = 64

A_KEY = 128
A_VAL = 128
A_WIDTH = 1536
A_HEADS = 12

B_WIDTH = 1024
B_HEAD_DIM = 128
B_HEADS = 8
B_ROT = 32
IDX_HEADS = 16
IDX_DIM = 64
IDX_ROT = 16
IDX_TOPK_MAX = 256
ROPE_THETA = 500000.0

C_WIDTH = 1536
C_VAL = 256
C_KEY = 128
C_HEADS = 6
RET_THETA = 10000.0

N_EXPERTS = 32
TOP_K = 4
D_EXPERT = 768
SWIGLU_LIMIT = 7.0
SWIGLU_ALPHA = 1.702
MOE_BLOCK = 256

DN_ALPHA = (2 * DEPTH) ** 0.25
LN_EPS = 1e-5
NORM_EPS = 1e-6

OFF_AQ, OFF_AF, OFF_AI, OFF_AG = 0, 12, 24, 36
OFF_BQ, OFF_BK, OFF_BV, OFF_IQ = 48, 56, 64, 72
OFF_CQ, OFF_CK, OFF_CV, OFF_CG, OFF_IKW = 80, 86, 92, 104, 116
PROJ_BLOCKS = 117
PROJ_PAD = PROJ_BLOCKS * LANES
IKW_START, IKW_END = 10240, 10320

INT_MIN = -2 ** 31
COUNT_ROWS = 128


def _cparams(sem, vmem=VMEM_LIMIT):
    return pltpu.CompilerParams(dimension_semantics=sem, vmem_limit_bytes=vmem)


def _ln_rows(x, g, b):
    mu = jnp.mean(x, axis=-1, keepdims=True)
    xc = x - mu
    var = jnp.mean(xc * xc, axis=-1, keepdims=True)
    return xc * lax.rsqrt(var + LN_EPS) * g + b


def _ln_in_kernel(x_ref, g_ref, b_ref, o_ref, ob_ref):
    y = _ln_rows(x_ref[...], g_ref[...], b_ref[...])
    o_ref[...] = y
    ob_ref[...] = y.astype(BF16)


def ln_in(x, g, b, rows=256):
    t, d = x.shape
    rows = min(rows, t)
    row_spec = pl.BlockSpec((rows, d), lambda i: (i, 0))
    vec_spec = pl.BlockSpec((1, d), lambda i: (0, 0))
    return pl.pallas_call(
        _ln_in_kernel,
        grid=(t // rows,),
        in_specs=[row_spec, vec_spec, vec_spec],
        out_specs=[row_spec, row_spec],
        out_shape=[jax.ShapeDtypeStruct((t, d), F32), jax.ShapeDtypeStruct((t, d), BF16)],
        compiler_params=_cparams(("parallel",)),
        name="ln_in",
    )(x, g.reshape(1, d), b.reshape(1, d))


def _ln1_router_kernel(h_ref, mm_ref, g_ref, b_ref, rw_ref, rb_ref, o_ref, lg_ref):
    y = _ln_rows(DN_ALPHA * h_ref[...] + mm_ref[...], g_ref[...], b_ref[...])
    o_ref[...] = y
    lg_ref[...] = jnp.dot(y.astype(BF16), rw_ref[...], preferred_element_type=F32) + rb_ref[...]


def ln1_router(h, mm, g, b, rw, rb, rows=256):
    t, d = h.shape
    rows = min(rows, t)
    row_spec = pl.BlockSpec((rows, d), lambda i: (i, 0))
    vec_spec = pl.BlockSpec((1, d), lambda i: (0, 0))
    return pl.pallas_call(
        _ln1_router_kernel,
        grid=(t // rows,),
        in_specs=[row_spec, row_spec, vec_spec, vec_spec,
                  pl.BlockSpec((d, LANES), lambda i: (0, 0)),
                  pl.BlockSpec((1, LANES), lambda i: (0, 0))],
        out_specs=[row_spec, pl.BlockSpec((rows, LANES), lambda i: (i, 0))],
        out_shape=[jax.ShapeDtypeStruct((t, d), F32), jax.ShapeDtypeStruct((t, LANES), F32)],
        compiler_params=_cparams(("parallel",)),
        name="ln1_router",
    )(h, mm, g.reshape(1, d), b.reshape(1, d), rw, rb)


def _matmul_kernel(a_ref, b_ref, o_ref):
    o_ref[...] = jnp.dot(a_ref[...], b_ref[...], preferred_element_type=F32)


def matmul(a, b, tm, tn):
    m, k = a.shape
    n = b.shape[1]
    tm = min(tm, m)
    return pl.pallas_call(
        _matmul_kernel,
        grid=(n // tn, m // tm),
        in_specs=[pl.BlockSpec((tm, k), lambda j, i: (i, 0)),
                  pl.BlockSpec((k, tn), lambda j, i: (0, j))],
        out_specs=pl.BlockSpec((tm, tn), lambda j, i: (i, j)),
        out_shape=jax.ShapeDtypeStruct((m, n), F32),
        compiler_params=_cparams(("parallel", "parallel")),
        name="matmul",
    )(a, b)


def _hgrn_kernel(q_ref, f_ref, i_ref, g_ref, lb_ref, ng_ref, o_ref,
                 state_ref, cum_ref, kk_ref, acc_ref, *, n_chunks):
    @pl.when(pl.program_id(1) == 0)
    def _():
        state_ref[...] = jnp.zeros_like(state_ref)

    lb = lb_ref[...]
    ng = ng_ref[...]
    row = lax.broadcasted_iota(jnp.int32, (CHUNK, CHUNK), 0)
    col = lax.broadcasted_iota(jnp.int32, (CHUNK, CHUNK), 1)
    tri = (col <= row).astype(F32)
    sub = 8

    for c in range(n_chunks):
        r0 = c * CHUNK
        f = lb + (1.0 - lb) * jax.nn.sigmoid(f_ref[r0:r0 + CHUNK, :])
        kk = 1.0 - f
        cum = jnp.dot(tri, jnp.log(f), precision=HI, preferred_element_type=F32)
        qs = q_ref[r0:r0 + CHUNK, :] * (A_KEY ** -0.5)
        inter = lax.dot_general(qs * jnp.exp(cum), state_ref[...], (((1,), (1,)), ((), ())),
                                precision=HI, preferred_element_type=F32)
        cum_ref[...] = cum
        kk_ref[...] = kk
        acc_ref[...] = inter

        for sb in range(CHUNK // sub):
            t0 = sb * sub
            qs_t = qs[t0:, :]
            cum_t = cum[t0:, :]
            t_idx = t0 + lax.broadcasted_iota(jnp.int32, (CHUNK - t0, 1), 0)

            def body(s, acc_t, qs_t=qs_t, cum_t=cum_t, t_idx=t_idx, r0=r0):
                rc = cum_ref[pl.ds(s, 1), :]
                rk = kk_ref[pl.ds(s, 1), :]
                rv = i_ref[pl.ds(r0 + s, 1), :]
                dec = jnp.exp(jnp.minimum(cum_t - rc, 0.0))
                a = jnp.sum(qs_t * rk * dec, axis=1, keepdims=True)
                a = jnp.where(t_idx >= s, a, 0.0)
                return acc_t + a * rv

            acc_t = lax.fori_loop(t0, t0 + sub, body, jnp.zeros((CHUNK - t0, A_VAL), F32), unroll=True)
            acc_ref[t0:, :] += acc_t

        last = cum[CHUNK - 1:CHUNK, :]
        kd = kk * jnp.exp(last - cum)
        v = i_ref[r0:r0 + CHUNK, :]
        state_ref[...] = state_ref[...] * jnp.exp(last) + lax.dot_general(
            v, kd, (((0,), (0,)), ((), ())), precision=HI, preferred_element_type=F32)

        o = acc_ref[...]
        o = o * lax.rsqrt(jnp.mean(o * o, axis=-1, keepdims=True) + NORM_EPS) * ng
        g = g_ref[r0:r0 + CHUNK, :]
        o_ref[r0:r0 + CHUNK, :] = (o * (g * jax.nn.sigmoid(g))).astype(o_ref.dtype)


def hgrn_mixer(p, lb, ng, rows=256):
    t = p.shape[0]
    rows = min(rows, t)

    def seg(off):
        return pl.BlockSpec((rows, LANES), lambda h, r: (r, off + h))

    vec = pl.BlockSpec((1, LANES), lambda h, r: (0, h))
    return pl.pallas_call(
        functools.partial(_hgrn_kernel, n_chunks=rows // CHUNK),
        grid=(A_HEADS, t // rows),
        in_specs=[seg(OFF_AQ), seg(OFF_AF), seg(OFF_AI), seg(OFF_AG), vec, vec],
        out_specs=pl.BlockSpec((rows, LANES), lambda h, r: (r, h)),
        out_shape=jax.ShapeDtypeStruct((t, A_WIDTH), BF16),
        scratch_shapes=[pltpu.VMEM((A_VAL, A_KEY), F32), pltpu.VMEM((CHUNK, A_KEY), F32),
                        pltpu.VMEM((CHUNK, A_KEY), F32), pltpu.VMEM((CHUNK, A_VAL), F32)],
        compiler_params=_cparams(("parallel", "arbitrary")),
        name="hgrn_mixer",
    )(p, p, p, p, lb, ng)


def _ret_consts():
    lg = np.log(1.0 - np.exp(np.linspace(np.log(1.0 / 32), np.log(1.0 / 512), C_HEADS)))
    pos = np.arange(CHUNK, dtype=np.float64)
    rel = pos[:, None] - pos[None, :]
    dmat = np.where(rel >= 0, np.exp(lg[:, None, None] * np.maximum(rel, 0.0)), 0.0)
    q_dec = np.broadcast_to(np.exp(lg[:, None] * (pos + 1.0))[..., None], (C_HEADS, CHUNK, C_KEY))
    k_dec = np.broadcast_to(np.exp(lg[:, None] * (CHUNK - 1.0 - pos))[..., None], (C_HEADS, CHUNK, C_KEY))
    c_dec = np.broadcast_to(np.exp(lg * CHUNK)[:, None, None], (C_HEADS, 1, C_VAL))
    return tuple(jnp.asarray(np.ascontiguousarray(a), F32) for a in (dmat, q_dec, k_dec, c_dec))


def _ret_kernel(q_ref, k_ref, v_ref, g_ref, cos_ref, sin_ref, dm_ref, qd_ref, kd_ref, cd_ref, ng_ref,
                o_ref, state_ref, *, n_chunks):
    @pl.when(pl.program_id(1) == 0)
    def _():
        state_ref[...] = jnp.zeros_like(state_ref)

    dmat = dm_ref[...]
    q_dec = qd_ref[...]
    k_dec = kd_ref[...]
    c_dec = cd_ref[...]
    ng = ng_ref[...]
    for c in range(n_chunks):
        r0 = c * CHUNK
        cos = cos_ref[r0:r0 + CHUNK, :]
        sin = sin_ref[r0:r0 + CHUNK, :]
        q = q_ref[r0:r0 + CHUNK, :]
        k = k_ref[r0:r0 + CHUNK, :]
        q = q * cos + pltpu.roll(q, C_KEY // 2, 1) * sin
        k = (k * cos + pltpu.roll(k, C_KEY // 2, 1) * sin) * (C_KEY ** -0.5)
        v = v_ref[r0:r0 + CHUNK, :]
        s = lax.dot_general(q, k, (((1,), (1,)), ((), ())), precision=HI, preferred_element_type=F32)
        intra = jnp.dot(s * dmat, v, precision=HI, preferred_element_type=F32)
        inter = jnp.dot(q * q_dec, state_ref[...], precision=HI, preferred_element_type=F32)
        state_ref[...] = c_dec * state_ref[...] + lax.dot_general(
            k * k_dec, v, (((0,), (0,)), ((), ())), precision=HI, preferred_element_type=F32)
        o = intra + inter
        mu = jnp.mean(o, axis=-1, keepdims=True)
        oc = o - mu
        var = jnp.mean(oc * oc, axis=-1, keepdims=True)
        o = oc * lax.rsqrt(var + NORM_EPS) * ng
        g = g_ref[r0:r0 + CHUNK, :]
        o_ref[r0:r0 + CHUNK, :] = (o * (g * jax.nn.sigmoid(g))).astype(o_ref.dtype)


def retention_mixer(p, cos, sin, ng, rows=256):
    t = p.shape[0]
    rows = min(rows, t)
    dmat, q_dec, k_dec, c_dec = _ret_consts()
    tab = pl.BlockSpec((rows, C_KEY), lambda h, r: (r, 0))
    return pl.pallas_call(
        functools.partial(_ret_kernel, n_chunks=rows // CHUNK),
        grid=(C_HEADS, t // rows),
        in_specs=[pl.BlockSpec((rows, C_KEY), lambda h, r: (r, OFF_CQ + h)),
                  pl.BlockSpec((rows, C_KEY), lambda h, r: (r, OFF_CK + h)),
                  pl.BlockSpec((rows, C_VAL), lambda h, r: (r, OFF_CV // 2 + h)),
                  pl.BlockSpec((rows, C_VAL), lambda h, r: (r, OFF_CG // 2 + h)),
                  tab, tab,
                  pl.BlockSpec((None, CHUNK, CHUNK), lambda h, r: (h, 0, 0)),
                  pl.BlockSpec((None, CHUNK, C_KEY), lambda h, r: (h, 0, 0)),
                  pl.BlockSpec((None, CHUNK, C_KEY), lambda h, r: (h, 0, 0)),
                  pl.BlockSpec((None, 1, C_VAL), lambda h, r: (h, 0, 0)),
                  pl.BlockSpec((1, C_VAL), lambda h, r: (0, h))],
        out_specs=pl.BlockSpec((rows, C_VAL), lambda h, r: (r, h)),
        out_shape=jax.ShapeDtypeStruct((t, C_WIDTH), BF16),
        scratch_shapes=[pltpu.VMEM((C_KEY, C_VAL), F32)],
        compiler_params=_cparams(("parallel", "arbitrary")),
        name="retention_mixer",
    )(p, p, p, p, cos, sin, dmat, q_dec, k_dec, c_dec, ng)


def _rope_tables(t, n_rot, theta, width):
    half = n_rot // 2
    inv_freq = 1.0 / (theta ** (jnp.arange(0, n_rot, 2, dtype=F32) / n_rot))
    ang = jnp.arange(t, dtype=F32)[:, None] * inv_freq[None, :]
    c, s = jnp.cos(ang), jnp.sin(ang)
    z = jnp.zeros((t, width - n_rot), F32)
    zh = jnp.zeros((t, half), F32)
    cos = jnp.concatenate([c, c, jnp.ones((t, width - n_rot), F32)], axis=1)
    sin_lo = jnp.concatenate([-s, zh, z], axis=1)
    sin_hi = jnp.concatenate([zh, s, z], axis=1)
    rep = LANES // width
    return tuple(jnp.tile(a, (1, rep)) for a in (cos, sin_lo, sin_hi))


def _rope128(x, cos, sin_lo, sin_hi, half):
    return x * cos + pltpu.roll(x, LANES - half, 1) * sin_lo + pltpu.roll(x, half, 1) * sin_hi


def _dsa_prep_kernel(q_ref, k_ref, v_ref, iq_ref, ikw_ref, cb_ref, slb_ref, shb_ref, ci_ref, sli_ref, shi_ref,
                     qo_ref, ko_ref, vo_ref, iqo_ref, iko_ref, wo_ref):
    cb, slb, shb = cb_ref[...], slb_ref[...], shb_ref[...]
    ci, sli, shi = ci_ref[...], sli_ref[...], shi_ref[...]
    for h in range(B_HEADS):
        sl = slice(h * LANES, (h + 1) * LANES)
        qo_ref[:, sl] = _rope128(q_ref[:, sl], cb, slb, shb, B_ROT // 2).astype(BF16)
        ko_ref[:, sl] = _rope128(k_ref[:, sl], cb, slb, shb, B_ROT // 2).astype(BF16)
        iqo_ref[:, sl] = _rope128(iq_ref[:, sl], ci, sli, shi, IDX_ROT // 2).astype(BF16)
    vo_ref[...] = v_ref[...].astype(BF16)
    ikw = ikw_ref[...]
    iko_ref[...] = _rope128(ikw, ci, sli, shi, IDX_ROT // 2)[:, :IDX_DIM].astype(BF16)
    wo_ref[...] = ikw[:, IDX_DIM:IDX_DIM + IDX_HEADS] * (IDX_DIM ** -0.5 * IDX_HEADS ** -0.5)


def dsa_prep(p, rows=256):
    t = p.shape[0]
    rows = min(rows, t)
    tabs_b = _rope_tables(t, B_ROT, ROPE_THETA, B_HEAD_DIM)
    tabs_i = _rope_tables(t, IDX_ROT, ROPE_THETA, IDX_DIM)
    wide = lambda off: pl.BlockSpec((rows, B_WIDTH), lambda r: (r, off // 8))
    tab = pl.BlockSpec((rows, LANES), lambda r: (r, 0))
    out_wide = pl.BlockSpec((rows, B_WIDTH), lambda r: (r, 0))
    return pl.pallas_call(
        _dsa_prep_kernel,
        grid=(t // rows,),
        in_specs=[wide(OFF_BQ), wide(OFF_BK), wide(OFF_BV), wide(OFF_IQ),
                  pl.BlockSpec((rows, LANES), lambda r: (r, OFF_IKW))] + [tab] * 6,
        out_specs=[out_wide, out_wide, out_wide, out_wide,
                   pl.BlockSpec((rows, IDX_DIM), lambda r: (r, 0)),
                   pl.BlockSpec((rows, IDX_HEADS), lambda r: (r, 0))],
        out_shape=[jax.ShapeDtypeStruct((t, B_WIDTH), BF16)] * 4
        + [jax.ShapeDtypeStruct((t, IDX_DIM), BF16), jax.ShapeDtypeStruct((t, IDX_HEADS), F32)],
        compiler_params=_cparams(("parallel",)),
        name="dsa_prep",
    )(p, p, p, p, p, *tabs_b, *tabs_i)


def _dsa_kernel(q_ref, qi_ref, w_ref, kit_ref, k_ref, v_ref, o_ref,
                keys_ref, thr_ref, tie_ref, m_ref, l_ref, acc_ref, *, qb, kb, topk, pos_bits):
    i = pl.program_id(0)
    j = pl.program_id(1)
    nkb = ((i + 1) * qb + kb - 1) // kb

    @pl.when(j == 0)
    def _():
        q_pos = i * qb + lax.broadcasted_iota(jnp.int32, (qb, 1), 0)
        limit = (q_pos // CHUNK + 1) * CHUNK
        w = w_ref[...]

        def score_block(b, carry):
            off = pl.multiple_of(b * kb, kb)
            kit = kit_ref[:, pl.ds(off, kb)]
            sc = jnp.zeros((qb, kb), F32)
            for h in range(IDX_HEADS):
                lg = jnp.dot(qi_ref[:, h * IDX_DIM:(h + 1) * IDX_DIM], kit, preferred_element_type=F32)
                sc = sc + jnp.maximum(lg, 0.0) * w[:, h:h + 1]
            bits = pltpu.bitcast(sc, jnp.int32)
            key = jnp.where(bits < 0, bits ^ jnp.int32(0x7FFFFFFF), bits)
            k_pos = off + lax.broadcasted_iota(jnp.int32, (1, kb), 1)
            keys_ref[:, pl.ds(off, kb)] = jnp.where(k_pos < limit, key, jnp.int32(INT_MIN))
            return carry

        lax.fori_loop(0, nkb, score_block, 0)

        def count_rows(make_hit):
            cnts = []
            for g in range(qb // COUNT_ROWS):
                rows = slice(g * COUNT_ROWS, (g + 1) * COUNT_ROWS)
                hit = make_hit(rows)

                def count_block(b, cnt, rows=rows, hit=hit):
                    off = pl.multiple_of(b * kb, kb)
                    for c in range(kb // LANES):
                        blk = keys_ref[rows, pl.ds(off + c * LANES, LANES)]
                        cnt = cnt + hit(blk, off + c * LANES).astype(jnp.int32)
                    return cnt

                cnts.append(lax.fori_loop(0, nkb, count_block, jnp.zeros((COUNT_ROWS, LANES), jnp.int32)))
            return jnp.sum(jnp.concatenate(cnts, axis=0), axis=1, keepdims=True)

        def count_ge(cand):
            cand_b = jnp.broadcast_to(cand, (qb, LANES))
            return count_rows(lambda rows: (lambda blk, pos, c=cand_b[rows]: blk >= c))

        def bit_step(it, thr):
            cand = thr ^ lax.shift_left(jnp.int32(1), 31 - it)
            return jnp.where(count_ge(cand) >= topk, cand, thr)

        thr = lax.fori_loop(0, 32, bit_step, jnp.full((qb, 1), INT_MIN, jnp.int32))
        thr = jnp.maximum(thr, jnp.int32(INT_MIN + 1))
        thr_ref[...] = thr

        n_ge = count_ge(thr)
        tie_ref[...] = jnp.full_like(tie_ref, 2 ** 31 - 1)

        @pl.when(jnp.max(n_ge) > topk)
        def _():
            need = topk - count_ge(thr + 1)
            thr_b = jnp.broadcast_to(thr, (qb, LANES))
            lane = lax.broadcasted_iota(jnp.int32, (COUNT_ROWS, LANES), 1)

            def pos_step(it, x):
                cand = x | lax.shift_left(jnp.int32(1), pos_bits - 1 - it)
                cand_b = jnp.broadcast_to(cand, (qb, LANES))
                tied_before = count_rows(
                    lambda rows: (lambda blk, pos, th=thr_b[rows], c=cand_b[rows]: (blk == th) & (lane < c - pos)))
                return jnp.where(tied_before < need, cand, x)

            x = lax.fori_loop(0, pos_bits, pos_step, jnp.zeros((qb, 1), jnp.int32))
            tie_ref[...] = jnp.where(n_ge > topk, x, 2 ** 31 - 1)

        m_ref[...] = jnp.full_like(m_ref, -1e30)
        l_ref[...] = jnp.zeros_like(l_ref)
        acc_ref[...] = jnp.zeros_like(acc_ref)

    @pl.when(j < nkb)
    def _():
        off = pl.multiple_of(j * kb, kb)
        key = keys_ref[:, pl.ds(off, kb)]
        thr = thr_ref[...]
        k_pos = off + lax.broadcasted_iota(jnp.int32, (1, kb), 1)
        mask = (key >= thr) & ((key > thr) | (k_pos <= tie_ref[...]))
        for h in range(B_HEADS):
            sl = slice(h * B_HEAD_DIM, (h + 1) * B_HEAD_DIM)
            s = lax.dot_general(q_ref[:, sl], k_ref[:, sl], (((1,), (1,)), ((), ())),
                                preferred_element_type=F32) * (B_HEAD_DIM ** -0.5)
            s = jnp.where(mask, s, -1e30)
            m_old = m_ref[h]
            m_new = jnp.maximum(m_old, jnp.max(s, axis=1, keepdims=True))
            alpha = jnp.exp(m_old - m_new)
            p = jnp.exp(s - m_new)
            l_ref[h] = alpha * l_ref[h] + jnp.sum(p, axis=1, keepdims=True)
            acc_ref[:, sl] = alpha * acc_ref[:, sl] + jnp.dot(p.astype(BF16), v_ref[:, sl],
                                                              preferred_element_type=F32)
            m_ref[h] = m_new

    @pl.when(j == nkb - 1)
    def _():
        for h in range(B_HEADS):
            sl = slice(h * B_HEAD_DIM, (h + 1) * B_HEAD_DIM)
            o_ref[:, sl] = (acc_ref[:, sl] / l_ref[h]).astype(o_ref.dtype)


def dsa_mixer(p, qb=256, kb=1024):
    t = p.shape[0]
    qb = min(qb, t)
    kb = min(kb, t)
    topk = min(IDX_TOPK_MAX, t // 4)
    q, k, v, qi, ki, wi = dsa_prep(p)
    kit = ki.T

    def kv_map(i, j):
        return (jnp.minimum(j, ((i + 1) * qb + kb - 1) // kb - 1), 0)

    q_spec = pl.BlockSpec((qb, B_WIDTH), lambda i, j: (i, 0))
    return pl.pallas_call(
        functools.partial(_dsa_kernel, qb=qb, kb=kb, topk=topk, pos_bits=max(1, (t - 1).bit_length())),
        grid=(t // qb, t // kb),
        in_specs=[q_spec, q_spec,
                  pl.BlockSpec((qb, IDX_HEADS), lambda i, j: (i, 0)),
                  pl.BlockSpec((IDX_DIM, t), lambda i, j: (0, 0)),
                  pl.BlockSpec((kb, B_WIDTH), kv_map),
                  pl.BlockSpec((kb, B_WIDTH), kv_map)],
        out_specs=q_spec,
        out_shape=jax.ShapeDtypeStruct((t, B_WIDTH), BF16),
        scratch_shapes=[pltpu.VMEM((qb, t), jnp.int32), pltpu.VMEM((qb, 1), jnp.int32),
                        pltpu.VMEM((qb, 1), jnp.int32),
                        pltpu.VMEM((B_HEADS, qb, 1), F32), pltpu.VMEM((B_HEADS, qb, 1), F32),
                        pltpu.VMEM((qb, B_WIDTH), F32)],
        compiler_params=_cparams(("arbitrary", "arbitrary")),
        name="dsa_mixer",
    )(q, qi, wi, kit, k, v)


EXPERT_F_TILES = 3


def _expert_kernel(be_ref, nu_ref, first_ref, nv_ref, tok_ref, h_hbm, wg_ref, wu_ref, bg_ref, bu_ref, wd_ref,
                   bd_ref, o_ref, xbuf_ref, xb_ref, act_ref, sem):
    b = pl.program_id(0)
    f = pl.program_id(1)
    n_used = nu_ref[0]
    used = b < n_used

    def row_copy(r, src_row):
        return pltpu.make_async_copy(h_hbm.at[pl.ds(src_row, 1), :], xbuf_ref.at[pl.ds(r, 1), :], sem)

    def issue_block(blk):
        first = first_ref[blk]
        last_valid = nv_ref[blk] - 1

        def issue(r, carry):
            row_copy(r, tok_ref[first + jnp.minimum(r, last_valid)]).start()
            return carry
        lax.fori_loop(0, MOE_BLOCK, issue, 0, unroll=8)

    @pl.when(jnp.logical_not(used) & (f == 0))
    def _():
        o_ref[...] = jnp.zeros_like(o_ref)

    @pl.when(used & (f == 0))
    def _():
        @pl.when(b == 0)
        def _():
            issue_block(0)

        def drain(r, carry):
            row_copy(r, 0).wait()
            return carry
        lax.fori_loop(0, MOE_BLOCK, drain, 0, unroll=8)
        xb_ref[...] = xbuf_ref[...].astype(BF16)

        @pl.when(b + 1 < n_used)
        def _():
            issue_block(b + 1)

    @pl.when(used)
    def _():
        x = xb_ref[...]
        g_lin = jnp.dot(x, wg_ref[...], preferred_element_type=F32) + bg_ref[...]
        u_lin = jnp.dot(x, wu_ref[...], preferred_element_type=F32) + bu_ref[...]
        g_lin = jnp.minimum(g_lin, SWIGLU_LIMIT)
        u_lin = jnp.clip(u_lin, -SWIGLU_LIMIT, SWIGLU_LIMIT)
        act = (u_lin + 1.0) * g_lin * jax.nn.sigmoid(SWIGLU_ALPHA * g_lin)
        ft = D_EXPERT // EXPERT_F_TILES
        tile = jnp.where(b % 2 == 0, f, EXPERT_F_TILES - 1 - f)
        act_ref[:, pl.ds(pl.multiple_of(tile * ft, ft), ft)] = act.astype(BF16)

        @pl.when(f == EXPERT_F_TILES - 1)
        def _():
            o_ref[...] = jnp.dot(act_ref[...], wd_ref[...], preferred_element_type=F32) + bd_ref[...]


def expert_ffn(h, tok_sorted, block_expert, first, n_valid, n_used, w_gu, b_gu, w_d, b_d):
    d = h.shape[1]
    n_blocks = block_expert.shape[0]
    ft = D_EXPERT // EXPERT_F_TILES

    def expert(b, be, nu):
        return be[jnp.minimum(b, nu[0] - 1)]

    def tile(b, f, nu):
        f = jnp.where(b < nu[0], f, EXPERT_F_TILES - 1)
        return jnp.where(jnp.minimum(b, nu[0] - 1) % 2 == 0, f, EXPERT_F_TILES - 1 - f)

    return pl.pallas_call(
        _expert_kernel,
        grid_spec=pltpu.PrefetchScalarGridSpec(
            num_scalar_prefetch=5,
            grid=(n_blocks, EXPERT_F_TILES),
            in_specs=[pl.BlockSpec(memory_space=pl.ANY),
                      pl.BlockSpec((None, d, ft), lambda b, f, be, nu, *_: (expert(b, be, nu), 0, tile(b, f, nu))),
                      pl.BlockSpec((None, d, ft),
                                   lambda b, f, be, nu, *_: (expert(b, be, nu), 0, EXPERT_F_TILES + tile(b, f, nu))),
                      pl.BlockSpec((None, 1, ft), lambda b, f, be, nu, *_: (expert(b, be, nu), 0, tile(b, f, nu))),
                      pl.BlockSpec((None, 1, ft),
                                   lambda b, f, be, nu, *_: (expert(b, be, nu), 0, EXPERT_F_TILES + tile(b, f, nu))),
                      pl.BlockSpec((None, D_EXPERT, d), lambda b, f, be, nu, *_: (expert(b, be, nu), 0, 0)),
                      pl.BlockSpec((None, 1, d), lambda b, f, be, nu, *_: (expert(b, be, nu), 0, 0))],
            out_specs=pl.BlockSpec((MOE_BLOCK, d), lambda b, f, *_: (b, 0)),
            scratch_shapes=[pltpu.VMEM((MOE_BLOCK, d), F32), pltpu.VMEM((MOE_BLOCK, d), BF16),
                            pltpu.VMEM((MOE_BLOCK, D_EXPERT), BF16), pltpu.SemaphoreType.DMA(())]),
        out_shape=jax.ShapeDtypeStruct((n_blocks * MOE_BLOCK, d), F32),
        compiler_params=_cparams(("arbitrary", "arbitrary")),
        name="moe_experts",
    )(block_expert, n_used, first, n_valid, tok_sorted, h, w_gu, w_gu, b_gu, b_gu, w_d, b_d)


def _combine_ln_kernel(pos_ref, ys_hbm, gate_ref, h_ref, g_ref, b_ref, o_ref, ob_ref, buf_ref, sem, *, rows):
    i = pl.program_id(0)

    def row_copy(tile, r, k, src_row):
        slot = tile % 2
        return pltpu.make_async_copy(ys_hbm.at[pl.ds(src_row, 1), :], buf_ref.at[slot, k, pl.ds(r, 1), :],
                                     sem.at[slot])

    def issue_tile(tile):
        def issue(r, carry):
            for k in range(TOP_K):
                row_copy(tile, r, k, pos_ref[(tile * rows + r) * TOP_K + k]).start()
            return carry
        lax.fori_loop(0, rows, issue, 0, unroll=4)

    @pl.when(i == 0)
    def _():
        issue_tile(0)

    @pl.when(i + 1 < pl.num_programs(0))
    def _():
        issue_tile(i + 1)

    def drain(r, carry):
        for k in range(TOP_K):
            row_copy(i, r, k, 0).wait()
        return carry

    lax.fori_loop(0, rows, drain, 0)
    slot = i % 2
    gate = gate_ref[...]
    ffn = ((buf_ref[slot, 0] * gate[:, 0:1] + buf_ref[slot, 1] * gate[:, 1:2])
           + (buf_ref[slot, 2] * gate[:, 2:3] + buf_ref[slot, 3] * gate[:, 3:4]))
    y = _ln_rows(DN_ALPHA * h_ref[...] + ffn, g_ref[...], b_ref[...])
    o_ref[...] = y
    ob_ref[...] = y.astype(BF16)


def combine_ln(ys, pos, gates, h, g, b, rows=64):
    t, d = h.shape
    rows = min(rows, t)
    row_spec = pl.BlockSpec((rows, d), lambda i, pos: (i, 0))
    vec_spec = pl.BlockSpec((1, d), lambda i, pos: (0, 0))
    return pl.pallas_call(
        functools.partial(_combine_ln_kernel, rows=rows),
        grid_spec=pltpu.PrefetchScalarGridSpec(
            num_scalar_prefetch=1,
            grid=(t // rows,),
            in_specs=[pl.BlockSpec(memory_space=pl.ANY), pl.BlockSpec((rows, TOP_K), lambda i, pos: (i, 0)),
                      row_spec, vec_spec, vec_spec],
            out_specs=[row_spec, row_spec],
            scratch_shapes=[pltpu.VMEM((2, TOP_K, rows, d), F32), pltpu.SemaphoreType.DMA((2,))]),
        out_shape=[jax.ShapeDtypeStruct((t, d), F32), jax.ShapeDtypeStruct((t, d), BF16)],
        compiler_params=_cparams(("arbitrary",)),
        name="moe_combine_ln",
    )(pos, ys, gates, h, g.reshape(1, d), b.reshape(1, d))


def moe_routing(logits):
    n_tok = logits.shape[0]
    n_assign = n_tok * TOP_K
    n_blocks = -(-n_assign // MOE_BLOCK) + N_EXPERTS
    top_val, top_idx = lax.top_k(logits, TOP_K)
    gates = jax.nn.softmax(top_val, axis=-1)
    flat_e = top_idx.reshape(-1).astype(jnp.int32)
    iota = jnp.arange(n_assign, dtype=jnp.int32)
    se, order = lax.sort((flat_e, iota), num_keys=1, is_stable=True)
    experts = jnp.arange(N_EXPERTS, dtype=jnp.int32)
    counts = jnp.sum((flat_e[:, None] == experts[None, :]).astype(jnp.int32), axis=0)
    padded = (counts + MOE_BLOCK - 1) // MOE_BLOCK * MOE_BLOCK
    start = jnp.cumsum(counts) - counts
    pend = jnp.cumsum(padded)
    shift = pend - padded - start
    dest = iota + jnp.sum(jnp.where(se[:, None] == experts[None, :], shift[None, :], 0), axis=1)
    _, pos = lax.sort((order, dest), num_keys=1)
    blk = jnp.arange(n_blocks, dtype=jnp.int32)
    block_expert = jnp.minimum(
        jnp.sum((blk[:, None] * MOE_BLOCK >= pend[None, :]).astype(jnp.int32), axis=1), N_EXPERTS - 1)
    first = blk * MOE_BLOCK - shift[block_expert]
    n_valid = jnp.clip(start[block_expert] + counts[block_expert] - first, 0, MOE_BLOCK)
    n_used = (pend[-1:] // MOE_BLOCK).astype(jnp.int32)
    return order // TOP_K, gates, pos, block_expert, first, n_valid, n_used


def _pack_w_in(w):
    d = w.shape[0]
    pad = jnp.zeros((d, LANES - (IKW_END - IKW_START)), w.dtype)
    return jnp.concatenate([w[:, :IKW_START], w[:, IKW_END:], w[:, IKW_START:IKW_END], pad], axis=1).astype(BF16)


def kernel(x, ln_in_g, ln_in_b, w_in, w_out, hgrn_lb, hgrn_norm_g, ret_norm_g, ln1_g, ln1_b,
           router_w, router_b, w_gate_up, b_gate_up, w_down, b_down, ln2_g, ln2_b):
    bsz, seq, d = x.shape
    assert bsz == 1
    t = seq
    lb_all = jnp.cumsum(jax.nn.softmax(hgrn_lb.astype(F32), axis=0), axis=0)
    lb_all = lb_all - lb_all[0:1]
    ang = jnp.arange(t, dtype=F32)[:, None] * (
        1.0 / (RET_THETA ** (jnp.arange(0, C_KEY, 2, dtype=F32) / C_KEY)))[None, :]
    cos_r = jnp.concatenate([jnp.cos(ang), jnp.cos(ang)], axis=1)
    sin_r = jnp.concatenate([-jnp.sin(ang), jnp.sin(ang)], axis=1)

    h, hb = ln_in(x.reshape(t, d), ln_in_g, ln_in_b)
    for l in range(DEPTH):
        p = matmul(hb, _pack_w_in(w_in[l]), tm=512, tn=9 * LANES)
        oa = hgrn_mixer(p, lb_all[l].reshape(1, A_WIDTH), hgrn_norm_g[l].reshape(1, A_WIDTH))
        ob = dsa_mixer(p)
        oc = retention_mixer(p, cos_r, sin_r, ret_norm_g[l].reshape(1, C_WIDTH))
        mix = jnp.concatenate([oa, ob, oc], axis=1)
        mm = matmul(mix, w_out[l].astype(BF16), tm=512, tn=1024)
        rw = jnp.pad(router_w[l], ((0, 0), (0, LANES - N_EXPERTS))).astype(BF16)
        rb = jnp.pad(router_b[l], (0, LANES - N_EXPERTS)).reshape(1, LANES)
        h1, logits = ln1_router(h, mm, ln1_g[l], ln1_b[l], rw, rb)
        tok_sorted, gates, pos, block_expert, first, n_valid, n_used = moe_routing(logits[:, :N_EXPERTS])
        ys = expert_ffn(h1, tok_sorted, block_expert, first, n_valid, n_used,
                        w_gate_up[l].astype(BF16), b_gate_up[l].reshape(N_EXPERTS, 1, 2 * D_EXPERT),
                        w_down[l].astype(BF16), b_down[l].reshape(N_EXPERTS, 1, d))
        h, hb = combine_ln(ys, pos, gates, h1, ln2_g[l], ln2_b[l])
    return h.reshape(bsz, t, d)
```

```python
import functools
import math

import numpy as np
import jax
import jax.numpy as jnp
from jax import lax
from jax.experimental import pallas as pl
from jax.experimental.pallas import tpu as pltpu

F32 = jnp.float32
BF16 = jnp.bfloat16
HI = lax.Precision.HIGHEST

LANES = 128
VMEM_LIMIT = 56 * 1024 * 1024

D_MODEL = 4096
DEPTH = 2
CHUNK = 64

A_KEY = 128
A_VAL = 128
A_WIDTH = 1536
A_HEADS = 12

B_WIDTH = 1024
B_HEAD_DIM = 128
B_HEADS = 8
B_ROT = 32
IDX_HEADS = 16
IDX_DIM = 64
IDX_ROT = 16
IDX_TOPK_MAX = 256
ROPE_THETA = 500000.0

C_WIDTH = 1536
C_VAL = 256
C_KEY = 128
C_HEADS = 6
RET_THETA = 10000.0

N_EXPERTS = 32
TOP_K = 4
D_EXPERT = 768
SWIGLU_LIMIT = 7.0
SWIGLU_ALPHA = 1.702
MOE_BLOCK = 512

DN_ALPHA = (2 * DEPTH) ** 0.25
LN_EPS = 1e-5
NORM_EPS = 1e-6

OFF_AQ, OFF_AF, OFF_AI, OFF_AG = 0, 12, 24, 36
OFF_BQ, OFF_BK, OFF_BV, OFF_IQ = 48, 56, 64, 72
OFF_CQ, OFF_CK, OFF_CV, OFF_CG, OFF_IKW = 80, 86, 92, 104, 116
PROJ_BLOCKS = 117
PROJ_PAD = PROJ_BLOCKS * LANES
IKW_START, IKW_END = 10240, 10320

INT_MIN = -2 ** 31
HGRN_BLOCK = 16
COUNT_ROWS = 128


def _cparams(sem, vmem=VMEM_LIMIT):
    return pltpu.CompilerParams(dimension_semantics=sem, vmem_limit_bytes=vmem)


def _ln_rows(x, g, b):
    mu = jnp.mean(x, axis=-1, keepdims=True)
    xc = x - mu
    var = jnp.mean(xc * xc, axis=-1, keepdims=True)
    return xc * lax.rsqrt(var + LN_EPS) * g + b


def _ln_in_kernel(x_ref, g_ref, b_ref, o_ref, ob_ref):
    y = _ln_rows(x_ref[...], g_ref[...], b_ref[...])
    o_ref[...] = y
    ob_ref[...] = y.astype(BF16)


def ln_in(x, g, b, rows=256):
    t, d = x.shape
    rows = min(rows, t)
    row_spec = pl.BlockSpec((rows, d), lambda i: (i, 0))
    vec_spec = pl.BlockSpec((1, d), lambda i: (0, 0))
    return pl.pallas_call(
        _ln_in_kernel,
        grid=(t // rows,),
        in_specs=[row_spec, vec_spec, vec_spec],
        out_specs=[row_spec, row_spec],
        out_shape=[jax.ShapeDtypeStruct((t, d), F32), jax.ShapeDtypeStruct((t, d), BF16)],
        compiler_params=_cparams(("parallel",)),
        name="ln_in",
    )(x, g.reshape(1, d), b.reshape(1, d))


def _ln1_router_kernel(h_ref, mm_ref, g_ref, b_ref, rw_ref, rb_ref, o_ref, lg_ref):
    y = _ln_rows(DN_ALPHA * h_ref[...] + mm_ref[...], g_ref[...], b_ref[...])
    o_ref[...] = y
    lg_ref[...] = jnp.dot(y.astype(BF16), rw_ref[...], preferred_element_type=F32) + rb_ref[...]


def ln1_router(h, mm, g, b, rw, rb, rows=256):
    t, d = h.shape
    rows = min(rows, t)
    row_spec = pl.BlockSpec((rows, d), lambda i: (i, 0))
    vec_spec = pl.BlockSpec((1, d), lambda i: (0, 0))
    return pl.pallas_call(
        _ln1_router_kernel,
        grid=(t // rows,),
        in_specs=[row_spec, row_spec, vec_spec, vec_spec,
                  pl.BlockSpec((d, LANES), lambda i: (0, 0)),
                  pl.BlockSpec((1, LANES), lambda i: (0, 0))],
        out_specs=[row_spec, pl.BlockSpec((rows, LANES), lambda i: (i, 0))],
        out_shape=[jax.ShapeDtypeStruct((t, d), F32), jax.ShapeDtypeStruct((t, LANES), F32)],
        compiler_params=_cparams(("parallel",)),
        name="ln1_router",
    )(h, mm, g.reshape(1, d), b.reshape(1, d), rw, rb)


def _matmul_kernel(a_ref, b_ref, o_ref):
    o_ref[...] = jnp.dot(a_ref[...], b_ref[...], preferred_element_type=F32)


def matmul(a, b, tm, tn):
    m, k = a.shape
    n = b.shape[1]
    tm = min(tm, m)
    return pl.pallas_call(
        _matmul_kernel,
        grid=(n // tn, m // tm),
        in_specs=[pl.BlockSpec((tm, k), lambda j, i: (i, 0)),
                  pl.BlockSpec((k, tn), lambda j, i: (0, j))],
        out_specs=pl.BlockSpec((tm, tn), lambda j, i: (i, j)),
        out_shape=jax.ShapeDtypeStruct((m, n), F32),
        compiler_params=_cparams(("parallel", "parallel")),
        name="matmul",
    )(a, b)


def _hgrn_kernel(q_ref, f_ref, i_ref, g_ref, lb_ref, ng_ref, o_ref,
                 state_ref, cum_ref, kk_ref, acc_ref, *, n_chunks):
    @pl.when(pl.program_id(1) == 0)
    def _():
        state_ref[...] = jnp.zeros_like(state_ref)

    lb = lb_ref[...]
    ng = ng_ref[...]
    row = lax.broadcasted_iota(jnp.int32, (CHUNK, CHUNK), 0)
    col = lax.broadcasted_iota(jnp.int32, (CHUNK, CHUNK), 1)
    tri = (col <= row).astype(F32)
    sub = 8

    for c in range(n_chunks):
        r0 = c * CHUNK
        f = lb + (1.0 - lb) * jax.nn.sigmoid(f_ref[r0:r0 + CHUNK, :])
        kk = 1.0 - f
        cum = jnp.dot(tri, jnp.log(f), precision=HI, preferred_element_type=F32)
        qs = q_ref[r0:r0 + CHUNK, :] * (A_KEY ** -0.5)
        inter = lax.dot_general(qs * jnp.exp(cum), state_ref[...], (((1,), (1,)), ((), ())),
                                precision=HI, preferred_element_type=F32)
        cum_ref[...] = cum
        kk_ref[...] = kk
        acc_ref[...] = inter

        v = i_ref[r0:r0 + CHUNK, :]

        for blk in range(1, CHUNK // HGRN_BLOCK):
            b0 = blk * HGRN_BLOCK
            cum_b = cum[b0 - 1:b0, :]
            qe = qs[b0:b0 + HGRN_BLOCK, :] * jnp.exp(cum[b0:b0 + HGRN_BLOCK, :] - cum_b)
            ke = kk[:b0, :] * jnp.exp(cum_b - cum[:b0, :])
            att = lax.dot_general(qe, ke, (((1,), (1,)), ((), ())), precision=HI, preferred_element_type=F32)
            acc_ref[b0:b0 + HGRN_BLOCK, :] += jnp.dot(att, v[:b0, :], precision=HI, preferred_element_type=F32)

        for sb in range(CHUNK // sub):
            t0 = sb * sub
            t1 = (t0 // HGRN_BLOCK + 1) * HGRN_BLOCK
            qs_t = qs[t0:t1, :]
            cum_t = cum[t0:t1, :]
            t_idx = t0 + lax.broadcasted_iota(jnp.int32, (t1 - t0, 1), 0)

            def body(s, acc_t, qs_t=qs_t, cum_t=cum_t, t_idx=t_idx, r0=r0):
                rc = cum_ref[pl.ds(s, 1), :]
                rk = kk_ref[pl.ds(s, 1), :]
                rv = i_ref[pl.ds(r0 + s, 1), :]
                dec = jnp.exp(jnp.minimum(cum_t - rc, 0.0))
                a = jnp.sum(qs_t * rk * dec, axis=1, keepdims=True)
                a = jnp.where(t_idx >= s, a, 0.0)
                return acc_t + a * rv

            acc_t = lax.fori_loop(t0, t0 + sub, body, jnp.zeros((t1 - t0, A_VAL), F32), unroll=True)
            acc_ref[t0:t1, :] += acc_t

        last = cum[CHUNK - 1:CHUNK, :]
        kd = kk * jnp.exp(last - cum)
        state_ref[...] = state_ref[...] * jnp.exp(last) + lax.dot_general(
            v, kd, (((0,), (0,)), ((), ())), precision=HI, preferred_element_type=F32)

        o = acc_ref[...]
        o = o * lax.rsqrt(jnp.mean(o * o, axis=-1, keepdims=True) + NORM_EPS) * ng
        g = g_ref[r0:r0 + CHUNK, :]
        o_ref[r0:r0 + CHUNK, :] = (o * (g * jax.nn.sigmoid(g))).astype(o_ref.dtype)


def hgrn_mixer(p, lb, ng, rows=256):
    t = p.shape[0]
    rows = min(rows, t)

    def seg(off):
        return pl.BlockSpec((rows, LANES), lambda h, r: (r, off + h))

    vec = pl.BlockSpec((1, LANES), lambda h, r: (0, h))
    return pl.pallas_call(
        functools.partial(_hgrn_kernel, n_chunks=rows // CHUNK),
        grid=(A_HEADS, t // rows),
        in_specs=[seg(OFF_AQ), seg(OFF_AF), seg(OFF_AI), seg(OFF_AG), vec, vec],
        out_specs=pl.BlockSpec((rows, LANES), lambda h, r: (r, h)),
        out_shape=jax.ShapeDtypeStruct((t, A_WIDTH), BF16),
        scratch_shapes=[pltpu.VMEM((A_VAL, A_KEY), F32), pltpu.VMEM((CHUNK, A_KEY), F32),
                        pltpu.VMEM((CHUNK, A_KEY), F32), pltpu.VMEM((CHUNK, A_VAL), F32)],
        compiler_params=_cparams(("parallel", "arbitrary")),
        name="hgrn_mixer",
    )(p, p, p, p, lb, ng)


def _ret_consts():
    lg = np.log(1.0 - np.exp(np.linspace(np.log(1.0 / 32), np.log(1.0 / 512), C_HEADS)))
    pos = np.arange(CHUNK, dtype=np.float64)
    rel = pos[:, None] - pos[None, :]
    dmat = np.where(rel >= 0, np.exp(lg[:, None, None] * np.maximum(rel, 0.0)), 0.0)
    q_dec = np.broadcast_to(np.exp(lg[:, None] * (pos + 1.0))[..., None], (C_HEADS, CHUNK, C_KEY))
    k_dec = np.broadcast_to(np.exp(lg[:, None] * (CHUNK - 1.0 - pos))[..., None], (C_HEADS, CHUNK, C_KEY))
    c_dec = np.broadcast_to(np.exp(lg * CHUNK)[:, None, None], (C_HEADS, 1, C_VAL))
    return tuple(jnp.asarray(np.ascontiguousarray(a), F32) for a in (dmat, q_dec, k_dec, c_dec))


def _ret_kernel(q_ref, k_ref, v_ref, g_ref, cos_ref, sin_ref, dm_ref, qd_ref, kd_ref, cd_ref, ng_ref,
                o_ref, state_ref, *, n_chunks):
    @pl.when(pl.program_id(1) == 0)
    def _():
        state_ref[...] = jnp.zeros_like(state_ref)

    dmat = dm_ref[...]
    q_dec = qd_ref[...]
    k_dec = kd_ref[...]
    c_dec = cd_ref[...]
    ng = ng_ref[...]
    for c in range(n_chunks):
        r0 = c * CHUNK
        cos = cos_ref[r0:r0 + CHUNK, :]
        sin = sin_ref[r0:r0 + CHUNK, :]
        q = q_ref[r0:r0 + CHUNK, :]
        k = k_ref[r0:r0 + CHUNK, :]
        q = q * cos + pltpu.roll(q, C_KEY // 2, 1) * sin
        k = (k * cos + pltpu.roll(k, C_KEY // 2, 1) * sin) * (C_KEY ** -0.5)
        v = v_ref[r0:r0 + CHUNK, :]
        s = lax.dot_general(q, k, (((1,), (1,)), ((), ())), precision=HI, preferred_element_type=F32)
        intra = jnp.dot(s * dmat, v, precision=HI, preferred_element_type=F32)
        inter = jnp.dot(q * q_dec, state_ref[...], precision=HI, preferred_element_type=F32)
        state_ref[...] = c_dec * state_ref[...] + lax.dot_general(
            k * k_dec, v, (((0,), (0,)), ((), ())), precision=HI, preferred_element_type=F32)
        o = intra + inter
        mu = jnp.mean(o, axis=-1, keepdims=True)
        oc = o - mu
        var = jnp.mean(oc * oc, axis=-1, keepdims=True)
        o = oc * lax.rsqrt(var + NORM_EPS) * ng
        g = g_ref[r0:r0 + CHUNK, :]
        o_ref[r0:r0 + CHUNK, :] = (o * (g * jax.nn.sigmoid(g))).astype(o_ref.dtype)


def retention_mixer(p, cos, sin, ng, rows=256):
    t = p.shape[0]
    rows = min(rows, t)
    dmat, q_dec, k_dec, c_dec = _ret_consts()
    tab = pl.BlockSpec((rows, C_KEY), lambda h, r: (r, 0))
    return pl.pallas_call(
        functools.partial(_ret_kernel, n_chunks=rows // CHUNK),
        grid=(C_HEADS, t // rows),
        in_specs=[pl.BlockSpec((rows, C_KEY), lambda h, r: (r, OFF_CQ + h)),
                  pl.BlockSpec((rows, C_KEY), lambda h, r: (r, OFF_CK + h)),
                  pl.BlockSpec((rows, C_VAL), lambda h, r: (r, OFF_CV // 2 + h)),
                  pl.BlockSpec((rows, C_VAL), lambda h, r: (r, OFF_CG // 2 + h)),
                  tab, tab,
                  pl.BlockSpec((None, CHUNK, CHUNK), lambda h, r: (h, 0, 0)),
                  pl.BlockSpec((None, CHUNK, C_KEY), lambda h, r: (h, 0, 0)),
                  pl.BlockSpec((None, CHUNK, C_KEY), lambda h, r: (h, 0, 0)),
                  pl.BlockSpec((None, 1, C_VAL), lambda h, r: (h, 0, 0)),
                  pl.BlockSpec((1, C_VAL), lambda h, r: (0, h))],
        out_specs=pl.BlockSpec((rows, C_VAL), lambda h, r: (r, h)),
        out_shape=jax.ShapeDtypeStruct((t, C_WIDTH), BF16),
        scratch_shapes=[pltpu.VMEM((C_KEY, C_VAL), F32)],
        compiler_params=_cparams(("parallel", "arbitrary")),
        name="retention_mixer",
    )(p, p, p, p, cos, sin, dmat, q_dec, k_dec, c_dec, ng)


def _rope_tables(t, n_rot, theta, width):
    half = n_rot // 2
    inv_freq = 1.0 / (theta ** (jnp.arange(0, n_rot, 2, dtype=F32) / n_rot))
    ang = jnp.arange(t, dtype=F32)[:, None] * inv_freq[None, :]
    c, s = jnp.cos(ang), jnp.sin(ang)
    z = jnp.zeros((t, width - n_rot), F32)
    zh = jnp.zeros((t, half), F32)
    cos = jnp.concatenate([c, c, jnp.ones((t, width - n_rot), F32)], axis=1)
    sin_lo = jnp.concatenate([-s, zh, z], axis=1)
    sin_hi = jnp.concatenate([zh, s, z], axis=1)
    rep = LANES // width
    return tuple(jnp.tile(a, (1, rep)) for a in (cos, sin_lo, sin_hi))


def _rope128(x, cos, sin_lo, sin_hi, half):
    return x * cos + pltpu.roll(x, LANES - half, 1) * sin_lo + pltpu.roll(x, half, 1) * sin_hi


def _dsa_prep_kernel(q_ref, k_ref, v_ref, iq_ref, ikw_ref, cb_ref, slb_ref, shb_ref, ci_ref, sli_ref, shi_ref,
                     qo_ref, ko_ref, vo_ref, iqo_ref, iko_ref, wo_ref):
    cb, slb, shb = cb_ref[...], slb_ref[...], shb_ref[...]
    ci, sli, shi = ci_ref[...], sli_ref[...], shi_ref[...]
    for h in range(B_HEADS):
        sl = slice(h * LANES, (h + 1) * LANES)
        qo_ref[:, sl] = _rope128(q_ref[:, sl], cb, slb, shb, B_ROT // 2).astype(BF16)
        ko_ref[:, sl] = _rope128(k_ref[:, sl], cb, slb, shb, B_ROT // 2).astype(BF16)
        iqo_ref[:, sl] = _rope128(iq_ref[:, sl], ci, sli, shi, IDX_ROT // 2).astype(BF16)
    vo_ref[...] = v_ref[...].astype(BF16)
    ikw = ikw_ref[...]
    iko_ref[...] = _rope128(ikw, ci, sli, shi, IDX_ROT // 2)[:, :IDX_DIM].astype(BF16)
    wo_ref[...] = ikw[:, IDX_DIM:IDX_DIM + IDX_HEADS] * (IDX_DIM ** -0.5 * IDX_HEADS ** -0.5)


def dsa_prep(p, rows=256):
    t = p.shape[0]
    rows = min(rows, t)
    tabs_b = _rope_tables(t, B_ROT, ROPE_THETA, B_HEAD_DIM)
    tabs_i = _rope_tables(t, IDX_ROT, ROPE_THETA, IDX_DIM)
    wide = lambda off: pl.BlockSpec((rows, B_WIDTH), lambda r: (r, off // 8))
    tab = pl.BlockSpec((rows, LANES), lambda r: (r, 0))
    out_wide = pl.BlockSpec((rows, B_WIDTH), lambda r: (r, 0))
    return pl.pallas_call(
        _dsa_prep_kernel,
        grid=(t // rows,),
        in_specs=[wide(OFF_BQ), wide(OFF_BK), wide(OFF_BV), wide(OFF_IQ),
                  pl.BlockSpec((rows, LANES), lambda r: (r, OFF_IKW))] + [tab] * 6,
        out_specs=[out_wide, out_wide, out_wide, out_wide,
                   pl.BlockSpec((rows, IDX_DIM), lambda r: (r, 0)),
                   pl.BlockSpec((rows, IDX_HEADS), lambda r: (r, 0))],
        out_shape=[jax.ShapeDtypeStruct((t, B_WIDTH), BF16)] * 4
        + [jax.ShapeDtypeStruct((t, IDX_DIM), BF16), jax.ShapeDtypeStruct((t, IDX_HEADS), F32)],
        compiler_params=_cparams(("parallel",)),
        name="dsa_prep",
    )(p, p, p, p, p, *tabs_b, *tabs_i)


def _dsa_kernel(q_ref, qi_ref, w_ref, kit_ref, k_ref, v_ref, o_ref,
                keys_ref, thr_ref, tie_ref, m_ref, l_ref, acc_ref, *, qb, kb, topk, pos_bits):
    i = pl.program_id(0)
    j = pl.program_id(1)
    nkb = ((i + 1) * qb + kb - 1) // kb

    @pl.when(j == 0)
    def _():
        q_pos = i * qb + lax.broadcasted_iota(jnp.int32, (qb, 1), 0)
        limit = (q_pos // CHUNK + 1) * CHUNK
        w = w_ref[...]

        def score_block(b, carry):
            off = pl.multiple_of(b * kb, kb)
            kit = kit_ref[:, pl.ds(off, kb)]
            sc = jnp.zeros((qb, kb), F32)
            for h in range(IDX_HEADS):
                lg = jnp.dot(qi_ref[:, h * IDX_DIM:(h + 1) * IDX_DIM], kit, preferred_element_type=F32)
                sc = sc + jnp.maximum(lg, 0.0) * w[:, h:h + 1]
            bits = pltpu.bitcast(sc, jnp.int32)
            key = jnp.where(bits < 0, bits ^ jnp.int32(0x7FFFFFFF), bits)
            k_pos = off + lax.broadcasted_iota(jnp.int32, (1, kb), 1)
            keys_ref[:, pl.ds(off, kb)] = jnp.where(k_pos < limit, key, jnp.int32(INT_MIN))
            return carry

        lax.fori_loop(0, nkb, score_block, 0)

        def count_rows(make_hit):
            cnts = []
            for g in range(qb // COUNT_ROWS):
                rows = slice(g * COUNT_ROWS, (g + 1) * COUNT_ROWS)
                hit = make_hit(rows)

                def count_block(b, cnt, rows=rows, hit=hit):
                    off = pl.multiple_of(b * kb, kb)
                    for c in range(kb // LANES):
                        blk = keys_ref[rows, pl.ds(off + c * LANES, LANES)]
                        cnt = cnt + hit(blk, off + c * LANES).astype(jnp.int32)
                    return cnt

                cnts.append(lax.fori_loop(0, nkb, count_block, jnp.zeros((COUNT_ROWS, LANES), jnp.int32)))
            return jnp.sum(jnp.concatenate(cnts, axis=0), axis=1, keepdims=True)

        def count_ge(cand):
            cand_b = jnp.broadcast_to(cand, (qb, LANES))
            return count_rows(lambda rows: (lambda blk, pos, c=cand_b[rows]: blk >= c))

        def bit_step(it, thr):
            cand = thr ^ lax.shift_left(jnp.int32(1), 31 - it)
            return jnp.where(count_ge(cand) >= topk, cand, thr)

        thr = lax.fori_loop(0, 32, bit_step, jnp.full((qb, 1), INT_MIN, jnp.int32))
        thr = jnp.maximum(thr, jnp.int32(INT_MIN + 1))
        thr_ref[...] = thr

        n_ge = count_ge(thr)
        tie_ref[...] = jnp.full_like(tie_ref, 2 ** 31 - 1)

        @pl.when(jnp.max(n_ge) > topk)
        def _():
            need = topk - count_ge(thr + 1)
            thr_b = jnp.broadcast_to(thr, (qb, LANES))
            lane = lax.broadcasted_iota(jnp.int32, (COUNT_ROWS, LANES), 1)

            def pos_step(it, x):
                cand = x | lax.shift_left(jnp.int32(1), pos_bits - 1 - it)
                cand_b = jnp.broadcast_to(cand, (qb, LANES))
                tied_before = count_rows(
                    lambda rows: (lambda blk, pos, th=thr_b[rows], c=cand_b[rows]: (blk == th) & (lane < c - pos)))
                return jnp.where(tied_before < need, cand, x)

            x = lax.fori_loop(0, pos_bits, pos_step, jnp.zeros((qb, 1), jnp.int32))
            tie_ref[...] = jnp.where(n_ge > topk, x, 2 ** 31 - 1)

        m_ref[...] = jnp.full_like(m_ref, -1e30)
        l_ref[...] = jnp.zeros_like(l_ref)
        acc_ref[...] = jnp.zeros_like(acc_ref)

    @pl.when(j < nkb)
    def _():
        off = pl.multiple_of(j * kb, kb)
        key = keys_ref[:, pl.ds(off, kb)]
        thr = thr_ref[...]
        k_pos = off + lax.broadcasted_iota(jnp.int32, (1, kb), 1)
        mask = (key >= thr) & ((key > thr) | (k_pos <= tie_ref[...]))
        for h in range(B_HEADS):
            sl = slice(h * B_HEAD_DIM, (h + 1) * B_HEAD_DIM)
            s = lax.dot_general(q_ref[:, sl], k_ref[:, sl], (((1,), (1,)), ((), ())),
                                preferred_element_type=F32) * (B_HEAD_DIM ** -0.5)
            s = jnp.where(mask, s, -1e30)
            m_old = m_ref[h]
            m_new = jnp.maximum(m_old, jnp.max(s, axis=1, keepdims=True))
            alpha = jnp.exp(m_old - m_new)
            p = jnp.exp(s - m_new)
            l_ref[h] = alpha * l_ref[h] + jnp.sum(p, axis=1, keepdims=True)
            acc_ref[:, sl] = alpha * acc_ref[:, sl] + jnp.dot(p.astype(BF16), v_ref[:, sl],
                                                              preferred_element_type=F32)
            m_ref[h] = m_new

    @pl.when(j == nkb - 1)
    def _():
        for h in range(B_HEADS):
            sl = slice(h * B_HEAD_DIM, (h + 1) * B_HEAD_DIM)
            o_ref[:, sl] = (acc_ref[:, sl] / l_ref[h]).astype(o_ref.dtype)


def dsa_mixer(p, qb=256, kb=1024):
    t = p.shape[0]
    qb = min(qb, t)
    kb = min(kb, t)
    topk = min(IDX_TOPK_MAX, t // 4)
    q, k, v, qi, ki, wi = dsa_prep(p)
    kit = ki.T

    def kv_map(i, j):
        return (jnp.minimum(j, ((i + 1) * qb + kb - 1) // kb - 1), 0)

    q_spec = pl.BlockSpec((qb, B_WIDTH), lambda i, j: (i, 0))
    return pl.pallas_call(
        functools.partial(_dsa_kernel, qb=qb, kb=kb, topk=topk, pos_bits=max(1, (t - 1).bit_length())),
        grid=(t // qb, t // kb),
        in_specs=[q_spec, q_spec,
                  pl.BlockSpec((qb, IDX_HEADS), lambda i, j: (i, 0)),
                  pl.BlockSpec((IDX_DIM, t), lambda i, j: (0, 0)),
                  pl.BlockSpec((kb, B_WIDTH), kv_map),
                  pl.BlockSpec((kb, B_WIDTH), kv_map)],
        out_specs=q_spec,
        out_shape=jax.ShapeDtypeStruct((t, B_WIDTH), BF16),
        scratch_shapes=[pltpu.VMEM((qb, t), jnp.int32), pltpu.VMEM((qb, 1), jnp.int32),
                        pltpu.VMEM((qb, 1), jnp.int32),
                        pltpu.VMEM((B_HEADS, qb, 1), F32), pltpu.VMEM((B_HEADS, qb, 1), F32),
                        pltpu.VMEM((qb, B_WIDTH), F32)],
        compiler_params=_cparams(("arbitrary", "arbitrary")),
        name="dsa_mixer",
    )(q, qi, wi, kit, k, v)


EXPERT_F_TILES = 3
DOWN_COLS = 1024
EXPERT_VMEM_LIMIT = 60 * 1024 * 1024


def _expert_kernel(be_ref, nu_ref, first_ref, nv_ref, tok_ref, h_hbm, wg_ref, wu_ref, bg_ref, bu_ref, wd_ref,
                   bd_ref, o_ref, xbuf_ref, xb_ref, act_ref, sem):
    b = pl.program_id(0)
    f = pl.program_id(1)
    n_used = nu_ref[0]
    used = b < n_used

    def row_copy(r, src_row):
        return pltpu.make_async_copy(h_hbm.at[pl.ds(src_row, 1), :], xbuf_ref.at[pl.ds(r, 1), :], sem)

    def issue_block(blk):
        first = first_ref[blk]
        last_valid = nv_ref[blk] - 1

        def issue(r, carry):
            row_copy(r, tok_ref[first + jnp.minimum(r, last_valid)]).start()
            return carry
        lax.fori_loop(0, MOE_BLOCK, issue, 0, unroll=8)

    @pl.when(jnp.logical_not(used) & (f == 0))
    def _():
        o_ref[...] = jnp.zeros_like(o_ref)

    @pl.when(used & (f == 0))
    def _():
        @pl.when(b == 0)
        def _():
            issue_block(0)

        def drain(r, carry):
            row_copy(r, 0).wait()
            return carry
        lax.fori_loop(0, MOE_BLOCK, drain, 0, unroll=8)
        xb_ref[...] = xbuf_ref[...].astype(BF16)

        @pl.when(b + 1 < n_used)
        def _():
            issue_block(b + 1)

    @pl.when(used)
    def _():
        x = xb_ref[...]
        g_lin = jnp.dot(x, wg_ref[...], preferred_element_type=F32) + bg_ref[...]
        u_lin = jnp.dot(x, wu_ref[...], preferred_element_type=F32) + bu_ref[...]
        g_lin = jnp.minimum(g_lin, SWIGLU_LIMIT)
        u_lin = jnp.clip(u_lin, -SWIGLU_LIMIT, SWIGLU_LIMIT)
        act = (u_lin + 1.0) * g_lin * jax.nn.sigmoid(SWIGLU_ALPHA * g_lin)
        ft = D_EXPERT // EXPERT_F_TILES
        tile = jnp.where(b % 2 == 0, f, EXPERT_F_TILES - 1 - f)
        act_ref[:, pl.ds(pl.multiple_of(tile * ft, ft), ft)] = act.astype(BF16)

        @pl.when(f == EXPERT_F_TILES - 1)
        def _():
            a = act_ref[...]
            for c in range(0, D_MODEL, DOWN_COLS):
                cols = slice(c, c + DOWN_COLS)
                o_ref[:, cols] = jnp.dot(a, wd_ref[:, cols], preferred_element_type=F32) + bd_ref[:, cols]


def expert_ffn(h, tok_sorted, block_expert, first, n_valid, n_used, w_gu, b_gu, w_d, b_d):
    d = h.shape[1]
    n_blocks = block_expert.shape[0]
    ft = D_EXPERT // EXPERT_F_TILES

    def expert(b, be, nu):
        return be[jnp.minimum(b, nu[0] - 1)]

    def tile(b, f, nu):
        f = jnp.where(b < nu[0], f, EXPERT_F_TILES - 1)
        return jnp.where(jnp.minimum(b, nu[0] - 1) % 2 == 0, f, EXPERT_F_TILES - 1 - f)

    return pl.pallas_call(
        _expert_kernel,
        grid_spec=pltpu.PrefetchScalarGridSpec(
            num_scalar_prefetch=5,
            grid=(n_blocks, EXPERT_F_TILES),
            in_specs=[pl.BlockSpec(memory_space=pl.ANY),
                      pl.BlockSpec((None, d, ft), lambda b, f, be, nu, *_: (expert(b, be, nu), 0, tile(b, f, nu))),
                      pl.BlockSpec((None, d, ft),
                                   lambda b, f, be, nu, *_: (expert(b, be, nu), 0, EXPERT_F_TILES + tile(b, f, nu))),
                      pl.BlockSpec((None, 1, ft), lambda b, f, be, nu, *_: (expert(b, be, nu), 0, tile(b, f, nu))),
                      pl.BlockSpec((None, 1, ft),
                                   lambda b, f, be, nu, *_: (expert(b, be, nu), 0, EXPERT_F_TILES + tile(b, f, nu))),
                      pl.BlockSpec((None, D_EXPERT, d), lambda b, f, be, nu, *_: (expert(b, be, nu), 0, 0)),
                      pl.BlockSpec((None, 1, d), lambda b, f, be, nu, *_: (expert(b, be, nu), 0, 0))],
            out_specs=pl.BlockSpec((MOE_BLOCK, d), lambda b, f, *_: (b, 0)),
            scratch_shapes=[pltpu.VMEM((MOE_BLOCK, d), F32), pltpu.VMEM((MOE_BLOCK, d), BF16),
                            pltpu.VMEM((MOE_BLOCK, D_EXPERT), BF16), pltpu.SemaphoreType.DMA(())]),
        out_shape=jax.ShapeDtypeStruct((n_blocks * MOE_BLOCK, d), F32),
        compiler_params=_cparams(("arbitrary", "arbitrary"), EXPERT_VMEM_LIMIT),
        name="moe_experts",
    )(block_expert, n_used, first, n_valid, tok_sorted, h, w_gu, w_gu, b_gu, b_gu, w_d, b_d)


def _combine_ln_kernel(pos_ref, ys_hbm, gate_ref, h_ref, g_ref, b_ref, o_ref, ob_ref, buf_ref, sem, *, rows):
    i = pl.program_id(0)

    def row_copy(tile, r, k, src_row):
        slot = tile % 2
        return pltpu.make_async_copy(ys_hbm.at[pl.ds(src_row, 1), :], buf_ref.at[slot, k, pl.ds(r, 1), :],
                                     sem.at[slot])

    def issue_tile(tile):
        def issue(r, carry):
            for k in range(TOP_K):
                row_copy(tile, r, k, pos_ref[(tile * rows + r) * TOP_K + k]).start()
            return carry
        lax.fori_loop(0, rows, issue, 0, unroll=4)

    @pl.when(i == 0)
    def _():
        issue_tile(0)

    @pl.when(i + 1 < pl.num_programs(0))
    def _():
        issue_tile(i + 1)

    def drain(r, carry):
        for k in range(TOP_K):
            row_copy(i, r, k, 0).wait()
        return carry

    lax.fori_loop(0, rows, drain, 0)
    slot = i % 2
    gate = gate_ref[...]
    ffn = ((buf_ref[slot, 0] * gate[:, 0:1] + buf_ref[slot, 1] * gate[:, 1:2])
           + (buf_ref[slot, 2] * gate[:, 2:3] + buf_ref[slot, 3] * gate[:, 3:4]))
    y = _ln_rows(DN_ALPHA * h_ref[...] + ffn, g_ref[...], b_ref[...])
    o_ref[...] = y
    ob_ref[...] = y.astype(BF16)


def combine_ln(ys, pos, gates, h, g, b, rows=64):
    t, d = h.shape
    rows = min(rows, t)
    row_spec = pl.BlockSpec((rows, d), lambda i, pos: (i, 0))
    vec_spec = pl.BlockSpec((1, d), lambda i, pos: (0, 0))
    return pl.pallas_call(
        functools.partial(_combine_ln_kernel, rows=rows),
        grid_spec=pltpu.PrefetchScalarGridSpec(
            num_scalar_prefetch=1,
            grid=(t // rows,),
            in_specs=[pl.BlockSpec(memory_space=pl.ANY), pl.BlockSpec((rows, TOP_K), lambda i, pos: (i, 0)),
                      row_spec, vec_spec, vec_spec],
            out_specs=[row_spec, row_spec],
            scratch_shapes=[pltpu.VMEM((2, TOP_K, rows, d), F32), pltpu.SemaphoreType.DMA((2,))]),
        out_shape=[jax.ShapeDtypeStruct((t, d), F32), jax.ShapeDtypeStruct((t, d), BF16)],
        compiler_params=_cparams(("arbitrary",)),
        name="moe_combine_ln",
    )(pos, ys, gates, h, g.reshape(1, d), b.reshape(1, d))


def moe_routing(logits):
    n_tok = logits.shape[0]
    n_assign = n_tok * TOP_K
    n_blocks = -(-n_assign // MOE_BLOCK) + N_EXPERTS
    top_val, top_idx = lax.top_k(logits, TOP_K)
    gates = jax.nn.softmax(top_val, axis=-1)
    flat_e = top_idx.reshape(-1).astype(jnp.int32)
    iota = jnp.arange(n_assign, dtype=jnp.int32)
    se, order = lax.sort((flat_e, iota), num_keys=1, is_stable=True)
    experts = jnp.arange(N_EXPERTS, dtype=jnp.int32)
    counts = jnp.sum((flat_e[:, None] == experts[None, :]).astype(jnp.int32), axis=0)
    padded = (counts + MOE_BLOCK - 1) // MOE_BLOCK * MOE_BLOCK
    start = jnp.cumsum(counts) - counts
    pend = jnp.cumsum(padded)
    shift = pend - padded - start
    dest = iota + jnp.sum(jnp.where(se[:, None] == experts[None, :], shift[None, :], 0), axis=1)
    _, pos = lax.sort((order, dest), num_keys=1)
    blk = jnp.arange(n_blocks, dtype=jnp.int32)
    block_expert = jnp.minimum(
        jnp.sum((blk[:, None] * MOE_BLOCK >= pend[None, :]).astype(jnp.int32), axis=1), N_EXPERTS - 1)
    first = blk * MOE_BLOCK - shift[block_expert]
    n_valid = jnp.clip(start[block_expert] + counts[block_expert] - first, 0, MOE_BLOCK)
    n_used = (pend[-1:] // MOE_BLOCK).astype(jnp.int32)
    return order // TOP_K, gates, pos, block_expert, first, n_valid, n_used


def _pack_w_in(w):
    d = w.shape[0]
    pad = jnp.zeros((d, LANES - (IKW_END - IKW_START)), w.dtype)
    return jnp.concatenate([w[:, :IKW_START], w[:, IKW_END:], w[:, IKW_START:IKW_END], pad], axis=1).astype(BF16)


def kernel(x, ln_in_g, ln_in_b, w_in, w_out, hgrn_lb, hgrn_norm_g, ret_norm_g, ln1_g, ln1_b,
           router_w, router_b, w_gate_up, b_gate_up, w_down, b_down, ln2_g, ln2_b):
    bsz, seq, d = x.shape
    assert bsz == 1
    t = seq
    lb_all = jnp.cumsum(jax.nn.softmax(hgrn_lb.astype(F32), axis=0), axis=0)
    lb_all = lb_all - lb_all[0:1]
    ang = jnp.arange(t, dtype=F32)[:, None] * (
        1.0 / (RET_THETA ** (jnp.arange(0, C_KEY, 2, dtype=F32) / C_KEY)))[None, :]
    cos_r = jnp.concatenate([jnp.cos(ang), jnp.cos(ang)], axis=1)
    sin_r = jnp.concatenate([-jnp.sin(ang), jnp.sin(ang)], axis=1)

    h, hb = ln_in(x.reshape(t, d), ln_in_g, ln_in_b)
    for l in range(DEPTH):
        p = matmul(hb, _pack_w_in(w_in[l]), tm=512, tn=9 * LANES)
        oa = hgrn_mixer(p, lb_all[l].reshape(1, A_WIDTH), hgrn_norm_g[l].reshape(1, A_WIDTH))
        ob = dsa_mixer(p)
        oc = retention_mixer(p, cos_r, sin_r, ret_norm_g[l].reshape(1, C_WIDTH))
        mix = jnp.concatenate([oa, ob, oc], axis=1)
        mm = matmul(mix, w_out[l].astype(BF16), tm=512, tn=1024)
        rw = jnp.pad(router_w[l], ((0, 0), (0, LANES - N_EXPERTS))).astype(BF16)
        rb = jnp.pad(router_b[l], (0, LANES - N_EXPERTS)).reshape(1, LANES)
        h1, logits = ln1_router(h, mm, ln1_g[l], ln1_b[l], rw, rb)
        tok_sorted, gates, pos, block_expert, first, n_valid, n_used = moe_routing(logits[:, :N_EXPERTS])
        ys = expert_ffn(h1, tok_sorted, block_expert, first, n_valid, n_used,
                        w_gate_up[l].astype(BF16), b_gate_up[l].reshape(N_EXPERTS, 1, 2 * D_EXPERT),
                        w_down[l].astype(BF16), b_down[l].reshape(N_EXPERTS, 1, d))
        h, hb = combine_ln(ys, pos, gates, h1, ln2_g[l], ln2_b[l])
    return h.reshape(bsz, t, d)
```

```python
import functools
import math

import numpy as np
import jax
import jax.numpy as jnp
from jax import lax
from jax.experimental import pallas as pl
from jax.experimental.pallas import tpu as pltpu

F32 = jnp.float32
BF16 = jnp.bfloat16
HI = lax.Precision.HIGHEST

LANES = 128
VMEM_LIMIT = 56 * 1024 * 1024

D_MODEL = 4096
DEPTH = 2
CHUNK = 64

A_KEY = 128
A_VAL = 128
A_WIDTH = 1536
A_HEADS = 12

B_WIDTH = 1024
B_HEAD_DIM = 128
B_HEADS = 8
B_ROT = 32
IDX_HEADS = 16
IDX_DIM = 64
IDX_ROT = 16
IDX_TOPK_MAX = 256
ROPE_THETA = 500000.0

C_WIDTH = 1536
C_VAL = 256
C_KEY = 128
C_HEADS = 6
RET_THETA = 10000.0

N_EXPERTS = 32
TOP_K = 4
D_EXPERT = 768
SWIGLU_LIMIT = 7.0
SWIGLU_ALPHA = 1.702
MOE_BLOCK = 512

DN_ALPHA = (2 * DEPTH) ** 0.25
LN_EPS = 1e-5
NORM_EPS = 1e-6

OFF_AQ, OFF_AF, OFF_AI, OFF_AG = 0, 12, 24, 36
OFF_BQ, OFF_BK, OFF_BV, OFF_IQ = 48, 56, 64, 72
OFF_CQ, OFF_CK, OFF_CV, OFF_CG, OFF_IKW = 80, 86, 92, 104, 116
PROJ_BLOCKS = 117
PROJ_PAD = PROJ_BLOCKS * LANES
IKW_START, IKW_END = 10240, 10320

INT_MIN = -2 ** 31
PROJ_ROWS = 1024
COUNT_ROWS = 128


def _cparams(sem, vmem=VMEM_LIMIT):
    return pltpu.CompilerParams(dimension_semantics=sem, vmem_limit_bytes=vmem)


def _ln_rows(x, g, b):
    mu = jnp.mean(x, axis=-1, keepdims=True)
    xc = x - mu
    var = jnp.mean(xc * xc, axis=-1, keepdims=True)
    return xc * lax.rsqrt(var + LN_EPS) * g + b


def _ln_in_kernel(x_ref, g_ref, b_ref, o_ref, ob_ref):
    y = _ln_rows(x_ref[...], g_ref[...], b_ref[...])
    o_ref[...] = y
    ob_ref[...] = y.astype(BF16)


def ln_in(x, g, b, rows=256):
    t, d = x.shape
    rows = min(rows, t)
    row_spec = pl.BlockSpec((rows, d), lambda i: (i, 0))
    vec_spec = pl.BlockSpec((1, d), lambda i: (0, 0))
    return pl.pallas_call(
        _ln_in_kernel,
        grid=(t // rows,),
        in_specs=[row_spec, vec_spec, vec_spec],
        out_specs=[row_spec, row_spec],
        out_shape=[jax.ShapeDtypeStruct((t, d), F32), jax.ShapeDtypeStruct((t, d), BF16)],
        compiler_params=_cparams(("parallel",)),
        name="ln_in",
    )(x, g.reshape(1, d), b.reshape(1, d))


def _ln1_router_kernel(h_ref, mm_ref, g_ref, b_ref, rw_ref, rb_ref, o_ref, lg_ref):
    y = _ln_rows(DN_ALPHA * h_ref[...] + mm_ref[...], g_ref[...], b_ref[...])
    o_ref[...] = y
    lg_ref[...] = jnp.dot(y.astype(BF16), rw_ref[...], preferred_element_type=F32) + rb_ref[...]


def ln1_router(h, mm, g, b, rw, rb, rows=256):
    t, d = h.shape
    rows = min(rows, t)
    row_spec = pl.BlockSpec((rows, d), lambda i: (i, 0))
    vec_spec = pl.BlockSpec((1, d), lambda i: (0, 0))
    return pl.pallas_call(
        _ln1_router_kernel,
        grid=(t // rows,),
        in_specs=[row_spec, row_spec, vec_spec, vec_spec,
                  pl.BlockSpec((d, LANES), lambda i: (0, 0)),
                  pl.BlockSpec((1, LANES), lambda i: (0, 0))],
        out_specs=[row_spec, pl.BlockSpec((rows, LANES), lambda i: (i, 0))],
        out_shape=[jax.ShapeDtypeStruct((t, d), F32), jax.ShapeDtypeStruct((t, LANES), F32)],
        compiler_params=_cparams(("parallel",)),
        name="ln1_router",
    )(h, mm, g.reshape(1, d), b.reshape(1, d), rw, rb)


def _matmul_kernel(a_ref, b_ref, o_ref):
    o_ref[...] = jnp.dot(a_ref[...], b_ref[...], preferred_element_type=F32)


def matmul(a, b, tm, tn):
    m, k = a.shape
    n = b.shape[1]
    tm = min(tm, m)
    return pl.pallas_call(
        _matmul_kernel,
        grid=(n // tn, m // tm),
        in_specs=[pl.BlockSpec((tm, k), lambda j, i: (i, 0)),
                  pl.BlockSpec((k, tn), lambda j, i: (0, j))],
        out_specs=pl.BlockSpec((tm, tn), lambda j, i: (i, j)),
        out_shape=jax.ShapeDtypeStruct((m, n), F32),
        compiler_params=_cparams(("parallel", "parallel")),
        name="matmul",
    )(a, b)


def _out_proj_kernel(a_ref, b_ref, c_ref, w_ref, o_ref):
    o_ref[...] = (jnp.dot(a_ref[...], w_ref[:A_WIDTH, :], preferred_element_type=F32)
                  + jnp.dot(b_ref[...], w_ref[A_WIDTH:A_WIDTH + B_WIDTH, :], preferred_element_type=F32)
                  + jnp.dot(c_ref[...], w_ref[A_WIDTH + B_WIDTH:, :], preferred_element_type=F32))


def out_proj(oa, ob, oc, w, tm, tn):
    m = oa.shape[0]
    k, n = w.shape
    tm = min(tm, m)
    return pl.pallas_call(
        _out_proj_kernel,
        grid=(n // tn, m // tm),
        in_specs=[pl.BlockSpec((tm, A_WIDTH), lambda j, i: (i, 0)),
                  pl.BlockSpec((tm, B_WIDTH), lambda j, i: (i, 0)),
                  pl.BlockSpec((tm, C_WIDTH), lambda j, i: (i, 0)),
                  pl.BlockSpec((k, tn), lambda j, i: (0, j))],
        out_specs=pl.BlockSpec((tm, tn), lambda j, i: (i, j)),
        out_shape=jax.ShapeDtypeStruct((m, n), F32),
        compiler_params=_cparams(("parallel", "parallel")),
        name="out_proj",
    )(oa, ob, oc, w)


def _hgrn_kernel(q_ref, f_ref, i_ref, g_ref, lb_ref, ng_ref, o_ref,
                 state_ref, cum_ref, kk_ref, acc_ref, *, n_chunks):
    @pl.when(pl.program_id(1) == 0)
    def _():
        state_ref[...] = jnp.zeros_like(state_ref)

    lb = lb_ref[...]
    ng = ng_ref[...]
    row = lax.broadcasted_iota(jnp.int32, (CHUNK, CHUNK), 0)
    col = lax.broadcasted_iota(jnp.int32, (CHUNK, CHUNK), 1)
    tri = (col <= row).astype(F32)
    sub = 8

    for c in range(n_chunks):
        r0 = c * CHUNK
        f = lb + (1.0 - lb) * jax.nn.sigmoid(f_ref[r0:r0 + CHUNK, :])
        kk = 1.0 - f
        cum = jnp.dot(tri, jnp.log(f), precision=HI, preferred_element_type=F32)
        qs = q_ref[r0:r0 + CHUNK, :] * (A_KEY ** -0.5)
        inter = lax.dot_general(qs * jnp.exp(cum), state_ref[...], (((1,), (1,)), ((), ())),
                                precision=HI, preferred_element_type=F32)
        cum_ref[...] = cum
        kk_ref[...] = kk
        acc_ref[...] = inter

        v = i_ref[r0:r0 + CHUNK, :]

        for sb in range(CHUNK // sub):
            t0 = sb * sub
            t1 = CHUNK
            qs_t = qs[t0:t1, :]
            cum_t = cum[t0:t1, :]
            t_idx = t0 + lax.broadcasted_iota(jnp.int32, (t1 - t0, 1), 0)

            def body(s, acc_t, qs_t=qs_t, cum_t=cum_t, t_idx=t_idx, r0=r0):
                rc = cum_ref[pl.ds(s, 1), :]
                rk = kk_ref[pl.ds(s, 1), :]
                rv = i_ref[pl.ds(r0 + s, 1), :]
                dec = jnp.exp(jnp.minimum(cum_t - rc, 0.0))
                a = jnp.sum(qs_t * rk * dec, axis=1, keepdims=True)
                a = jnp.where(t_idx >= s, a, 0.0)
                return acc_t + a * rv

            acc_t = lax.fori_loop(t0, t0 + sub, body, jnp.zeros((t1 - t0, A_VAL), F32), unroll=True)
            acc_ref[t0:t1, :] += acc_t

        last = cum[CHUNK - 1:CHUNK, :]
        kd = kk * jnp.exp(last - cum)
        state_ref[...] = state_ref[...] * jnp.exp(last) + lax.dot_general(
            v, kd, (((0,), (0,)), ((), ())), precision=HI, preferred_element_type=F32)

        o = acc_ref[...]
        o = o * lax.rsqrt(jnp.mean(o * o, axis=-1, keepdims=True) + NORM_EPS) * ng
        g = g_ref[r0:r0 + CHUNK, :]
        o_ref[r0:r0 + CHUNK, :] = (o * (g * jax.nn.sigmoid(g))).astype(o_ref.dtype)


def hgrn_mixer(p, lb, ng, rows=256):
    t = p.shape[0]
    rows = min(rows, t)

    def seg(off):
        return pl.BlockSpec((rows, LANES), lambda h, r: (r, off + h))

    vec = pl.BlockSpec((1, LANES), lambda h, r: (0, h))
    return pl.pallas_call(
        functools.partial(_hgrn_kernel, n_chunks=rows // CHUNK),
        grid=(A_HEADS, t // rows),
        in_specs=[seg(OFF_AQ), seg(OFF_AF), seg(OFF_AI), seg(OFF_AG), vec, vec],
        out_specs=pl.BlockSpec((rows, LANES), lambda h, r: (r, h)),
        out_shape=jax.ShapeDtypeStruct((t, A_WIDTH), BF16),
        scratch_shapes=[pltpu.VMEM((A_VAL, A_KEY), F32), pltpu.VMEM((CHUNK, A_KEY), F32),
                        pltpu.VMEM((CHUNK, A_KEY), F32), pltpu.VMEM((CHUNK, A_VAL), F32)],
        compiler_params=_cparams(("parallel", "arbitrary")),
        name="hgrn_mixer",
    )(p, p, p, p, lb, ng)


def _ret_consts():
    lg = np.log(1.0 - np.exp(np.linspace(np.log(1.0 / 32), np.log(1.0 / 512), C_HEADS)))
    pos = np.arange(CHUNK, dtype=np.float64)
    rel = pos[:, None] - pos[None, :]
    dmat = np.where(rel >= 0, np.exp(lg[:, None, None] * np.maximum(rel, 0.0)), 0.0)
    q_dec = np.broadcast_to(np.exp(lg[:, None] * (pos + 1.0))[..., None], (C_HEADS, CHUNK, C_KEY))
    k_dec = np.broadcast_to(np.exp(lg[:, None] * (CHUNK - 1.0 - pos))[..., None], (C_HEADS, CHUNK, C_KEY))
    c_dec = np.broadcast_to(np.exp(lg * CHUNK)[:, None, None], (C_HEADS, 1, C_VAL))
    return tuple(jnp.asarray(np.ascontiguousarray(a), F32) for a in (dmat, q_dec, k_dec, c_dec))


def _ret_kernel(q_ref, k_ref, v_ref, g_ref, cos_ref, sin_ref, dm_ref, qd_ref, kd_ref, cd_ref, ng_ref,
                o_ref, state_ref, *, n_chunks):
    @pl.when(pl.program_id(1) == 0)
    def _():
        state_ref[...] = jnp.zeros_like(state_ref)

    dmat = dm_ref[...]
    q_dec = qd_ref[...]
    k_dec = kd_ref[...]
    c_dec = cd_ref[...]
    ng = ng_ref[...]
    for c in range(n_chunks):
        r0 = c * CHUNK
        cos = cos_ref[r0:r0 + CHUNK, :]
        sin = sin_ref[r0:r0 + CHUNK, :]
        q = q_ref[r0:r0 + CHUNK, :]
        k = k_ref[r0:r0 + CHUNK, :]
        q = q * cos + pltpu.roll(q, C_KEY // 2, 1) * sin
        k = (k * cos + pltpu.roll(k, C_KEY // 2, 1) * sin) * (C_KEY ** -0.5)
        v = v_ref[r0:r0 + CHUNK, :].astype(BF16)
        s = lax.dot_general(q.astype(BF16), k.astype(BF16), (((1,), (1,)), ((), ())), preferred_element_type=F32)
        intra = jnp.dot((s * dmat).astype(BF16), v, preferred_element_type=F32)
        inter = jnp.dot((q * q_dec).astype(BF16), state_ref[...].astype(BF16), preferred_element_type=F32)
        state_ref[...] = c_dec * state_ref[...] + jnp.dot(
            jnp.transpose(k * k_dec).astype(BF16), v, preferred_element_type=F32)
        o = intra + inter
        mu = jnp.mean(o, axis=-1, keepdims=True)
        oc = o - mu
        var = jnp.mean(oc * oc, axis=-1, keepdims=True)
        o = oc * lax.rsqrt(var + NORM_EPS) * ng
        g = g_ref[r0:r0 + CHUNK, :]
        o_ref[r0:r0 + CHUNK, :] = (o * (g * jax.nn.sigmoid(g))).astype(o_ref.dtype)


def retention_mixer(p, cos, sin, ng, rows=256):
    t = p.shape[0]
    rows = min(rows, t)
    dmat, q_dec, k_dec, c_dec = _ret_consts()
    tab = pl.BlockSpec((rows, C_KEY), lambda h, r: (r, 0))
    return pl.pallas_call(
        functools.partial(_ret_kernel, n_chunks=rows // CHUNK),
        grid=(C_HEADS, t // rows),
        in_specs=[pl.BlockSpec((rows, C_KEY), lambda h, r: (r, OFF_CQ + h)),
                  pl.BlockSpec((rows, C_KEY), lambda h, r: (r, OFF_CK + h)),
                  pl.BlockSpec((rows, C_VAL), lambda h, r: (r, OFF_CV // 2 + h)),
                  pl.BlockSpec((rows, C_VAL), lambda h, r: (r, OFF_CG // 2 + h)),
                  tab, tab,
                  pl.BlockSpec((None, CHUNK, CHUNK), lambda h, r: (h, 0, 0)),
                  pl.BlockSpec((None, CHUNK, C_KEY), lambda h, r: (h, 0, 0)),
                  pl.BlockSpec((None, CHUNK, C_KEY), lambda h, r: (h, 0, 0)),
                  pl.BlockSpec((None, 1, C_VAL), lambda h, r: (h, 0, 0)),
                  pl.BlockSpec((1, C_VAL), lambda h, r: (0, h))],
        out_specs=pl.BlockSpec((rows, C_VAL), lambda h, r: (r, h)),
        out_shape=jax.ShapeDtypeStruct((t, C_WIDTH), BF16),
        scratch_shapes=[pltpu.VMEM((C_KEY, C_VAL), F32)],
        compiler_params=_cparams(("parallel", "arbitrary")),
        name="retention_mixer",
    )(p, p, p, p, cos, sin, dmat, q_dec, k_dec, c_dec, ng)


def _rope_tables(t, n_rot, theta, width):
    half = n_rot // 2
    inv_freq = 1.0 / (theta ** (jnp.arange(0, n_rot, 2, dtype=F32) / n_rot))
    ang = jnp.arange(t, dtype=F32)[:, None] * inv_freq[None, :]
    c, s = jnp.cos(ang), jnp.sin(ang)
    z = jnp.zeros((t, width - n_rot), F32)
    zh = jnp.zeros((t, half), F32)
    cos = jnp.concatenate([c, c, jnp.ones((t, width - n_rot), F32)], axis=1)
    sin_lo = jnp.concatenate([-s, zh, z], axis=1)
    sin_hi = jnp.concatenate([zh, s, z], axis=1)
    rep = LANES // width
    return tuple(jnp.tile(a, (1, rep)) for a in (cos, sin_lo, sin_hi))


def _rope128(x, cos, sin_lo, sin_hi, half):
    return x * cos + pltpu.roll(x, LANES - half, 1) * sin_lo + pltpu.roll(x, half, 1) * sin_hi


def _dsa_prep_kernel(q_ref, k_ref, v_ref, iq_ref, ikw_ref, cb_ref, slb_ref, shb_ref, ci_ref, sli_ref, shi_ref,
                     qo_ref, ko_ref, vo_ref, iqo_ref, iko_ref, wo_ref):
    cb, slb, shb = cb_ref[...], slb_ref[...], shb_ref[...]
    ci, sli, shi = ci_ref[...], sli_ref[...], shi_ref[...]
    for h in range(B_HEADS):
        sl = slice(h * LANES, (h + 1) * LANES)
        qo_ref[:, sl] = _rope128(q_ref[:, sl], cb, slb, shb, B_ROT // 2).astype(BF16)
        ko_ref[:, sl] = _rope128(k_ref[:, sl], cb, slb, shb, B_ROT // 2).astype(BF16)
        iqo_ref[:, sl] = _rope128(iq_ref[:, sl], ci, sli, shi, IDX_ROT // 2).astype(BF16)
    vo_ref[...] = v_ref[...].astype(BF16)
    ikw = ikw_ref[...]
    iko_ref[...] = _rope128(ikw, ci, sli, shi, IDX_ROT // 2)[:, :IDX_DIM].astype(BF16)
    wo_ref[...] = ikw[:, IDX_DIM:IDX_DIM + IDX_HEADS] * (IDX_DIM ** -0.5 * IDX_HEADS ** -0.5)


def dsa_prep(p, rows=256):
    t = p.shape[0]
    rows = min(rows, t)
    tabs_b = _rope_tables(t, B_ROT, ROPE_THETA, B_HEAD_DIM)
    tabs_i = _rope_tables(t, IDX_ROT, ROPE_THETA, IDX_DIM)
    wide = lambda off: pl.BlockSpec((rows, B_WIDTH), lambda r: (r, off // 8))
    tab = pl.BlockSpec((rows, LANES), lambda r: (r, 0))
    out_wide = pl.BlockSpec((rows, B_WIDTH), lambda r: (r, 0))
    return pl.pallas_call(
        _dsa_prep_kernel,
        grid=(t // rows,),
        in_specs=[wide(OFF_BQ), wide(OFF_BK), wide(OFF_BV), wide(OFF_IQ),
                  pl.BlockSpec((rows, LANES), lambda r: (r, OFF_IKW))] + [tab] * 6,
        out_specs=[out_wide, out_wide, out_wide, out_wide,
                   pl.BlockSpec((rows, IDX_DIM), lambda r: (r, 0)),
                   pl.BlockSpec((rows, IDX_HEADS), lambda r: (r, 0))],
        out_shape=[jax.ShapeDtypeStruct((t, B_WIDTH), BF16)] * 4
        + [jax.ShapeDtypeStruct((t, IDX_DIM), BF16), jax.ShapeDtypeStruct((t, IDX_HEADS), F32)],
        compiler_params=_cparams(("parallel",)),
        name="dsa_prep",
    )(p, p, p, p, p, *tabs_b, *tabs_i)


def _dsa_kernel(q_ref, qi_ref, w_ref, kit_ref, k_ref, v_ref, o_ref,
                keys_ref, thr_ref, tie_ref, m_ref, l_ref, acc_ref, *, qb, kb, topk, pos_bits):
    i = pl.program_id(0)
    j = pl.program_id(1)
    nkb = ((i + 1) * qb + kb - 1) // kb

    @pl.when(j == 0)
    def _():
        q_pos = i * qb + lax.broadcasted_iota(jnp.int32, (qb, 1), 0)
        limit = (q_pos // CHUNK + 1) * CHUNK
        w = w_ref[...]

        def score_block(b, carry):
            off = pl.multiple_of(b * kb, kb)
            kit = kit_ref[:, pl.ds(off, kb)]
            sc = jnp.zeros((qb, kb), F32)
            for h in range(IDX_HEADS):
                lg = jnp.dot(qi_ref[:, h * IDX_DIM:(h + 1) * IDX_DIM], kit, preferred_element_type=F32)
                sc = sc + jnp.maximum(lg, 0.0) * w[:, h:h + 1]
            bits = pltpu.bitcast(sc, jnp.int32)
            key = jnp.where(bits < 0, bits ^ jnp.int32(0x7FFFFFFF), bits)
            k_pos = off + lax.broadcasted_iota(jnp.int32, (1, kb), 1)
            keys_ref[:, pl.ds(off, kb)] = jnp.where(k_pos < limit, key, jnp.int32(INT_MIN))
            return carry

        lax.fori_loop(0, nkb, score_block, 0)

        def count_rows(make_hit):
            cnts = []
            for g in range(qb // COUNT_ROWS):
                rows = slice(g * COUNT_ROWS, (g + 1) * COUNT_ROWS)
                hit = make_hit(rows)

                def count_block(b, cnt, rows=rows, hit=hit):
                    off = pl.multiple_of(b * kb, kb)
                    for c in range(kb // LANES):
                        blk = keys_ref[rows, pl.ds(off + c * LANES, LANES)]
                        cnt = cnt + hit(blk, off + c * LANES).astype(jnp.int32)
                    return cnt

                cnts.append(lax.fori_loop(0, nkb, count_block, jnp.zeros((COUNT_ROWS, LANES), jnp.int32)))
            return jnp.sum(jnp.concatenate(cnts, axis=0), axis=1, keepdims=True)

        def count_ge(cand):
            cand_b = jnp.broadcast_to(cand, (qb, LANES))
            return count_rows(lambda rows: (lambda blk, pos, c=cand_b[rows]: blk >= c))

        def bit_step(it, thr):
            cand = thr ^ lax.shift_left(jnp.int32(1), 31 - it)
            return jnp.where(count_ge(cand) >= topk, cand, thr)

        thr = lax.fori_loop(0, 32, bit_step, jnp.full((qb, 1), INT_MIN, jnp.int32))
        thr = jnp.maximum(thr, jnp.int32(INT_MIN + 1))
        thr_ref[...] = thr

        n_ge = count_ge(thr)
        tie_ref[...] = jnp.full_like(tie_ref, 2 ** 31 - 1)

        @pl.when(jnp.max(n_ge) > topk)
        def _():
            need = topk - count_ge(thr + 1)
            thr_b = jnp.broadcast_to(thr, (qb, LANES))
            lane = lax.broadcasted_iota(jnp.int32, (COUNT_ROWS, LANES), 1)

            def pos_step(it, x):
                cand = x | lax.shift_left(jnp.int32(1), pos_bits - 1 - it)
                cand_b = jnp.broadcast_to(cand, (qb, LANES))
                tied_before = count_rows(
                    lambda rows: (lambda blk, pos, th=thr_b[rows], c=cand_b[rows]: (blk == th) & (lane < c - pos)))
                return jnp.where(tied_before < need, cand, x)

            x = lax.fori_loop(0, pos_bits, pos_step, jnp.zeros((qb, 1), jnp.int32))
            tie_ref[...] = jnp.where(n_ge > topk, x, 2 ** 31 - 1)

        m_ref[...] = jnp.full_like(m_ref, -1e30)
        l_ref[...] = jnp.zeros_like(l_ref)
        acc_ref[...] = jnp.zeros_like(acc_ref)

    @pl.when(j < nkb)
    def _():
        off = pl.multiple_of(j * kb, kb)
        key = keys_ref[:, pl.ds(off, kb)]
        thr = thr_ref[...]
        k_pos = off + lax.broadcasted_iota(jnp.int32, (1, kb), 1)
        mask = (key >= thr) & ((key > thr) | (k_pos <= tie_ref[...]))
        for h in range(B_HEADS):
            sl = slice(h * B_HEAD_DIM, (h + 1) * B_HEAD_DIM)
            s = lax.dot_general(q_ref[:, sl], k_ref[:, sl], (((1,), (1,)), ((), ())),
                                preferred_element_type=F32) * (B_HEAD_DIM ** -0.5)
            s = jnp.where(mask, s, -1e30)
            m_old = m_ref[h]
            m_new = jnp.maximum(m_old, jnp.max(s, axis=1, keepdims=True))
            alpha = jnp.exp(m_old - m_new)
            p = jnp.exp(s - m_new)
            l_ref[h] = alpha * l_ref[h] + jnp.sum(p, axis=1, keepdims=True)
            acc_ref[:, sl] = alpha * acc_ref[:, sl] + jnp.dot(p.astype(BF16), v_ref[:, sl],
                                                              preferred_element_type=F32)
            m_ref[h] = m_new

    @pl.when(j == nkb - 1)
    def _():
        for h in range(B_HEADS):
            sl = slice(h * B_HEAD_DIM, (h + 1) * B_HEAD_DIM)
            o_ref[:, sl] = (acc_ref[:, sl] / l_ref[h]).astype(o_ref.dtype)


def dsa_mixer(p, qb=256, kb=1024):
    t = p.shape[0]
    qb = min(qb, t)
    kb = min(kb, t)
    topk = min(IDX_TOPK_MAX, t // 4)
    q, k, v, qi, ki, wi = dsa_prep(p)
    kit = ki.T

    def kv_map(i, j):
        return (jnp.minimum(j, ((i + 1) * qb + kb - 1) // kb - 1), 0)

    q_spec = pl.BlockSpec((qb, B_WIDTH), lambda i, j: (i, 0))
    return pl.pallas_call(
        functools.partial(_dsa_kernel, qb=qb, kb=kb, topk=topk, pos_bits=max(1, (t - 1).bit_length())),
        grid=(t // qb, t // kb),
        in_specs=[q_spec, q_spec,
                  pl.BlockSpec((qb, IDX_HEADS), lambda i, j: (i, 0)),
                  pl.BlockSpec((IDX_DIM, t), lambda i, j: (0, 0)),
                  pl.BlockSpec((kb, B_WIDTH), kv_map),
                  pl.BlockSpec((kb, B_WIDTH), kv_map)],
        out_specs=q_spec,
        out_shape=jax.ShapeDtypeStruct((t, B_WIDTH), BF16),
        scratch_shapes=[pltpu.VMEM((qb, t), jnp.int32), pltpu.VMEM((qb, 1), jnp.int32),
                        pltpu.VMEM((qb, 1), jnp.int32),
                        pltpu.VMEM((B_HEADS, qb, 1), F32), pltpu.VMEM((B_HEADS, qb, 1), F32),
                        pltpu.VMEM((qb, B_WIDTH), F32)],
        compiler_params=_cparams(("arbitrary", "arbitrary")),
        name="dsa_mixer",
    )(q, qi, wi, kit, k, v)


EXPERT_F_TILES = 3
DOWN_COLS = 1024
EXPERT_VMEM_LIMIT = 60 * 1024 * 1024


def _expert_kernel(be_ref, nu_ref, first_ref, nv_ref, tok_ref, h_hbm, wg_ref, wu_ref, bg_ref, bu_ref, wd_ref,
                   bd_ref, o_ref, xbuf_ref, xb_ref, act_ref, sem):
    b = pl.program_id(0)
    f = pl.program_id(1)
    n_used = nu_ref[0]
    used = b < n_used

    def row_copy(r, src_row):
        return pltpu.make_async_copy(h_hbm.at[pl.ds(src_row, 1), :], xbuf_ref.at[pl.ds(r, 1), :], sem)

    def issue_block(blk):
        first = first_ref[blk]
        last_valid = nv_ref[blk] - 1

        def issue(r, carry):
            row_copy(r, tok_ref[first + jnp.minimum(r, last_valid)]).start()
            return carry
        lax.fori_loop(0, MOE_BLOCK, issue, 0, unroll=8)

    @pl.when(jnp.logical_not(used) & (f == 0))
    def _():
        o_ref[...] = jnp.zeros_like(o_ref)

    @pl.when(used & (f == 0))
    def _():
        @pl.when(b == 0)
        def _():
            issue_block(0)

        def drain(r, carry):
            row_copy(r, 0).wait()
            return carry
        lax.fori_loop(0, MOE_BLOCK, drain, 0, unroll=8)
        xb_ref[...] = xbuf_ref[...].astype(BF16)

        @pl.when(b + 1 < n_used)
        def _():
            issue_block(b + 1)

    @pl.when(used)
    def _():
        x = xb_ref[...]
        g_lin = jnp.dot(x, wg_ref[...], preferred_element_type=F32) + bg_ref[...]
        u_lin = jnp.dot(x, wu_ref[...], preferred_element_type=F32) + bu_ref[...]
        g_lin = jnp.minimum(g_lin, SWIGLU_LIMIT)
        u_lin = jnp.clip(u_lin, -SWIGLU_LIMIT, SWIGLU_LIMIT)
        act = (u_lin + 1.0) * g_lin * jax.nn.sigmoid(SWIGLU_ALPHA * g_lin)
        ft = D_EXPERT // EXPERT_F_TILES
        tile = jnp.where(b % 2 == 0, f, EXPERT_F_TILES - 1 - f)
        act_ref[:, pl.ds(pl.multiple_of(tile * ft, ft), ft)] = act.astype(BF16)

        @pl.when(f == EXPERT_F_TILES - 1)
        def _():
            a = act_ref[...]
            for c in range(0, D_MODEL, DOWN_COLS):
                cols = slice(c, c + DOWN_COLS)
                o_ref[:, cols] = jnp.dot(a, wd_ref[:, cols], preferred_element_type=F32) + bd_ref[:, cols]


def expert_ffn(h, tok_sorted, block_expert, first, n_valid, n_used, w_gu, b_gu, w_d, b_d):
    d = h.shape[1]
    n_blocks = block_expert.shape[0]
    ft = D_EXPERT // EXPERT_F_TILES

    def expert(b, be, nu):
        return be[jnp.minimum(b, nu[0] - 1)]

    def tile(b, f, nu):
        f = jnp.where(b < nu[0], f, EXPERT_F_TILES - 1)
        return jnp.where(jnp.minimum(b, nu[0] - 1) % 2 == 0, f, EXPERT_F_TILES - 1 - f)

    return pl.pallas_call(
        _expert_kernel,
        grid_spec=pltpu.PrefetchScalarGridSpec(
            num_scalar_prefetch=5,
            grid=(n_blocks, EXPERT_F_TILES),
            in_specs=[pl.BlockSpec(memory_space=pl.ANY),
                      pl.BlockSpec((None, d, ft), lambda b, f, be, nu, *_: (expert(b, be, nu), 0, tile(b, f, nu))),
                      pl.BlockSpec((None, d, ft),
                                   lambda b, f, be, nu, *_: (expert(b, be, nu), 0, EXPERT_F_TILES + tile(b, f, nu))),
                      pl.BlockSpec((None, 1, ft), lambda b, f, be, nu, *_: (expert(b, be, nu), 0, tile(b, f, nu))),
                      pl.BlockSpec((None, 1, ft),
                                   lambda b, f, be, nu, *_: (expert(b, be, nu), 0, EXPERT_F_TILES + tile(b, f, nu))),
                      pl.BlockSpec((None, D_EXPERT, d), lambda b, f, be, nu, *_: (expert(b, be, nu), 0, 0)),
                      pl.BlockSpec((None, 1, d), lambda b, f, be, nu, *_: (expert(b, be, nu), 0, 0))],
            out_specs=pl.BlockSpec((MOE_BLOCK, d), lambda b, f, *_: (b, 0)),
            scratch_shapes=[pltpu.VMEM((MOE_BLOCK, d), F32), pltpu.VMEM((MOE_BLOCK, d), BF16),
                            pltpu.VMEM((MOE_BLOCK, D_EXPERT), BF16), pltpu.SemaphoreType.DMA(())]),
        out_shape=jax.ShapeDtypeStruct((n_blocks * MOE_BLOCK, d), F32),
        compiler_params=_cparams(("arbitrary", "arbitrary"), EXPERT_VMEM_LIMIT),
        name="moe_experts",
    )(block_expert, n_used, first, n_valid, tok_sorted, h, w_gu, w_gu, b_gu, b_gu, w_d, b_d)


def _combine_ln_kernel(pos_ref, ys_hbm, gate_ref, h_ref, g_ref, b_ref, o_ref, ob_ref, buf_ref, sem, *, rows):
    i = pl.program_id(0)

    def row_copy(tile, r, k, src_row):
        slot = tile % 2
        return pltpu.make_async_copy(ys_hbm.at[pl.ds(src_row, 1), :], buf_ref.at[slot, k, pl.ds(r, 1), :],
                                     sem.at[slot])

    def issue_tile(tile):
        def issue(r, carry):
            for k in range(TOP_K):
                row_copy(tile, r, k, pos_ref[(tile * rows + r) * TOP_K + k]).start()
            return carry
        lax.fori_loop(0, rows, issue, 0, unroll=4)

    @pl.when(i == 0)
    def _():
        issue_tile(0)

    @pl.when(i + 1 < pl.num_programs(0))
    def _():
        issue_tile(i + 1)

    def drain(r, carry):
        for k in range(TOP_K):
            row_copy(i, r, k, 0).wait()
        return carry

    lax.fori_loop(0, rows, drain, 0)
    slot = i % 2
    gate = gate_ref[...]
    ffn = ((buf_ref[slot, 0] * gate[:, 0:1] + buf_ref[slot, 1] * gate[:, 1:2])
           + (buf_ref[slot, 2] * gate[:, 2:3] + buf_ref[slot, 3] * gate[:, 3:4]))
    y = _ln_rows(DN_ALPHA * h_ref[...] + ffn, g_ref[...], b_ref[...])
    o_ref[...] = y
    ob_ref[...] = y.astype(BF16)


def combine_ln(ys, pos, gates, h, g, b, rows=64):
    t, d = h.shape
    rows = min(rows, t)
    row_spec = pl.BlockSpec((rows, d), lambda i, pos: (i, 0))
    vec_spec = pl.BlockSpec((1, d), lambda i, pos: (0, 0))
    return pl.pallas_call(
        functools.partial(_combine_ln_kernel, rows=rows),
        grid_spec=pltpu.PrefetchScalarGridSpec(
            num_scalar_prefetch=1,
            grid=(t // rows,),
            in_specs=[pl.BlockSpec(memory_space=pl.ANY), pl.BlockSpec((rows, TOP_K), lambda i, pos: (i, 0)),
                      row_spec, vec_spec, vec_spec],
            out_specs=[row_spec, row_spec],
            scratch_shapes=[pltpu.VMEM((2, TOP_K, rows, d), F32), pltpu.SemaphoreType.DMA((2,))]),
        out_shape=[jax.ShapeDtypeStruct((t, d), F32), jax.ShapeDtypeStruct((t, d), BF16)],
        compiler_params=_cparams(("arbitrary",)),
        name="moe_combine_ln",
    )(pos, ys, gates, h, g.reshape(1, d), b.reshape(1, d))


def moe_routing(logits):
    n_tok = logits.shape[0]
    n_assign = n_tok * TOP_K
    n_blocks = -(-n_assign // MOE_BLOCK) + N_EXPERTS
    top_val, top_idx = lax.top_k(logits, TOP_K)
    gates = jax.nn.softmax(top_val, axis=-1)
    flat_e = top_idx.reshape(-1).astype(jnp.int32)
    iota = jnp.arange(n_assign, dtype=jnp.int32)
    se, order = lax.sort((flat_e, iota), num_keys=1, is_stable=True)
    experts = jnp.arange(N_EXPERTS, dtype=jnp.int32)
    counts = jnp.sum((flat_e[:, None] == experts[None, :]).astype(jnp.int32), axis=0)
    padded = (counts + MOE_BLOCK - 1) // MOE_BLOCK * MOE_BLOCK
    start = jnp.cumsum(counts) - counts
    pend = jnp.cumsum(padded)
    shift = pend - padded - start
    dest = iota + jnp.sum(jnp.where(se[:, None] == experts[None, :], shift[None, :], 0), axis=1)
    _, pos = lax.sort((order, dest), num_keys=1)
    blk = jnp.arange(n_blocks, dtype=jnp.int32)
    block_expert = jnp.minimum(
        jnp.sum((blk[:, None] * MOE_BLOCK >= pend[None, :]).astype(jnp.int32), axis=1), N_EXPERTS - 1)
    first = blk * MOE_BLOCK - shift[block_expert]
    n_valid = jnp.clip(start[block_expert] + counts[block_expert] - first, 0, MOE_BLOCK)
    n_used = (pend[-1:] // MOE_BLOCK).astype(jnp.int32)
    return order // TOP_K, gates, pos, block_expert, first, n_valid, n_used


def _pack_w_in(w):
    d = w.shape[0]
    pad = jnp.zeros((d, LANES - (IKW_END - IKW_START)), w.dtype)
    return jnp.concatenate([w[:, :IKW_START], w[:, IKW_END:], w[:, IKW_START:IKW_END], pad], axis=1).astype(BF16)


def kernel(x, ln_in_g, ln_in_b, w_in, w_out, hgrn_lb, hgrn_norm_g, ret_norm_g, ln1_g, ln1_b,
           router_w, router_b, w_gate_up, b_gate_up, w_down, b_down, ln2_g, ln2_b):
    bsz, seq, d = x.shape
    assert bsz == 1
    t = seq
    lb_all = jnp.cumsum(jax.nn.softmax(hgrn_lb.astype(F32), axis=0), axis=0)
    lb_all = lb_all - lb_all[0:1]
    ang = jnp.arange(t, dtype=F32)[:, None] * (
        1.0 / (RET_THETA ** (jnp.arange(0, C_KEY, 2, dtype=F32) / C_KEY)))[None, :]
    cos_r = jnp.concatenate([jnp.cos(ang), jnp.cos(ang)], axis=1)
    sin_r = jnp.concatenate([-jnp.sin(ang), jnp.sin(ang)], axis=1)

    h, hb = ln_in(x.reshape(t, d), ln_in_g, ln_in_b)
    for l in range(DEPTH):
        p = matmul(hb, _pack_w_in(w_in[l]), tm=PROJ_ROWS, tn=9 * LANES)
        oa = hgrn_mixer(p, lb_all[l].reshape(1, A_WIDTH), hgrn_norm_g[l].reshape(1, A_WIDTH))
        ob = dsa_mixer(p)
        oc = retention_mixer(p, cos_r, sin_r, ret_norm_g[l].reshape(1, C_WIDTH))
        mm = out_proj(oa, ob, oc, w_out[l].astype(BF16), tm=PROJ_ROWS, tn=1024)
        rw = jnp.pad(router_w[l], ((0, 0), (0, LANES - N_EXPERTS))).astype(BF16)
        rb = jnp.pad(router_b[l], (0, LANES - N_EXPERTS)).reshape(1, LANES)
        h1, logits = ln1_router(h, mm, ln1_g[l], ln1_b[l], rw, rb)
        tok_sorted, gates, pos, block_expert, first, n_valid, n_used = moe_routing(logits[:, :N_EXPERTS])
        ys = expert_ffn(h1, tok_sorted, block_expert, first, n_valid, n_used,
                        w_gate_up[l].astype(BF16), b_gate_up[l].reshape(N_EXPERTS, 1, 2 * D_EXPERT),
                        w_down[l].astype(BF16), b_down[l].reshape(N_EXPERTS, 1, d))
        h, hb = combine_ln(ys, pos, gates, h1, ln2_g[l], ln2_b[l])
    return h.reshape(bsz, t, d)
```

```python
import functools
import math

import numpy as np
import jax
import jax.numpy as jnp
from jax import lax
from jax.experimental import pallas as pl
from jax.experimental.pallas import tpu as pltpu

F32 = jnp.float32
BF16 = jnp.bfloat16
HI = lax.Precision.HIGHEST

LANES = 128
VMEM_LIMIT = 56 * 1024 * 1024

D_MODEL = 4096
DEPTH = 2
CHUNK = 64

A_KEY = 128
A_VAL = 128
A_WIDTH = 1536
A_HEADS = 12

B_WIDTH = 1024
B_HEAD_DIM = 128
B_HEADS = 8
B_ROT = 32
IDX_HEADS = 16
IDX_DIM = 64
IDX_ROT = 16
IDX_TOPK_MAX = 256
ROPE_THETA = 500000.0

C_WIDTH = 1536
C_VAL = 256
C_KEY = 128
C_HEADS = 6
RET_THETA = 10000.0

N_EXPERTS = 32
TOP_K = 4
D_EXPERT = 768
SWIGLU_LIMIT = 7.0
SWIGLU_ALPHA = 1.702
MOE_BLOCK = 512

DN_ALPHA = (2 * DEPTH) ** 0.25
LN_EPS = 1e-5
NORM_EPS = 1e-6

OFF_AQ, OFF_AF, OFF_AI, OFF_AG = 0, 12, 24, 36
OFF_BQ, OFF_BK, OFF_BV, OFF_IQ = 48, 56, 64, 72
OFF_CQ, OFF_CK, OFF_CV, OFF_CG, OFF_IKW = 80, 86, 92, 104, 116
PROJ_BLOCKS = 117
PROJ_PAD = PROJ_BLOCKS * LANES
IKW_START, IKW_END = 10240, 10320

INT_MIN = -2 ** 31
PROJ_ROWS = 1024
COUNT_ROWS = 128


def _cparams(sem, vmem=VMEM_LIMIT):
    return pltpu.CompilerParams(dimension_semantics=sem, vmem_limit_bytes=vmem)


def _ln_rows(x, g, b):
    mu = jnp.mean(x, axis=-1, keepdims=True)
    xc = x - mu
    var = jnp.mean(xc * xc, axis=-1, keepdims=True)
    return xc * lax.rsqrt(var + LN_EPS) * g + b


def _ln_in_kernel(x_ref, g_ref, b_ref, o_ref, ob_ref):
    y = _ln_rows(x_ref[...], g_ref[...], b_ref[...])
    o_ref[...] = y
    ob_ref[...] = y.astype(BF16)


def ln_in(x, g, b, rows=256):
    t, d = x.shape
    rows = min(rows, t)
    row_spec = pl.BlockSpec((rows, d), lambda i: (i, 0))
    vec_spec = pl.BlockSpec((1, d), lambda i: (0, 0))
    return pl.pallas_call(
        _ln_in_kernel,
        grid=(t // rows,),
        in_specs=[row_spec, vec_spec, vec_spec],
        out_specs=[row_spec, row_spec],
        out_shape=[jax.ShapeDtypeStruct((t, d), F32), jax.ShapeDtypeStruct((t, d), BF16)],
        compiler_params=_cparams(("parallel",)),
        name="ln_in",
    )(x, g.reshape(1, d), b.reshape(1, d))


def _ln1_router_kernel(h_ref, mm_ref, g_ref, b_ref, rw_ref, rb_ref, o_ref, lg_ref):
    y = _ln_rows(DN_ALPHA * h_ref[...] + mm_ref[...], g_ref[...], b_ref[...])
    o_ref[...] = y
    lg_ref[...] = jnp.dot(y.astype(BF16), rw_ref[...], preferred_element_type=F32) + rb_ref[...]


def ln1_router(h, mm, g, b, rw, rb, rows=256):
    t, d = h.shape
    rows = min(rows, t)
    row_spec = pl.BlockSpec((rows, d), lambda i: (i, 0))
    vec_spec = pl.BlockSpec((1, d), lambda i: (0, 0))
    return pl.pallas_call(
        _ln1_router_kernel,
        grid=(t // rows,),
        in_specs=[row_spec, row_spec, vec_spec, vec_spec,
                  pl.BlockSpec((d, LANES), lambda i: (0, 0)),
                  pl.BlockSpec((1, LANES), lambda i: (0, 0))],
        out_specs=[row_spec, pl.BlockSpec((rows, LANES), lambda i: (i, 0))],
        out_shape=[jax.ShapeDtypeStruct((t, d), F32), jax.ShapeDtypeStruct((t, LANES), F32)],
        compiler_params=_cparams(("parallel",)),
        name="ln1_router",
    )(h, mm, g.reshape(1, d), b.reshape(1, d), rw, rb)


def _matmul_kernel(a_ref, b_ref, o_ref):
    o_ref[...] = jnp.dot(a_ref[...], b_ref[...], preferred_element_type=F32)


def matmul(a, b, tm, tn):
    m, k = a.shape
    n = b.shape[1]
    tm = min(tm, m)
    return pl.pallas_call(
        _matmul_kernel,
        grid=(n // tn, m // tm),
        in_specs=[pl.BlockSpec((tm, k), lambda j, i: (i, 0)),
                  pl.BlockSpec((k, tn), lambda j, i: (0, j))],
        out_specs=pl.BlockSpec((tm, tn), lambda j, i: (i, j)),
        out_shape=jax.ShapeDtypeStruct((m, n), F32),
        compiler_params=_cparams(("parallel", "parallel")),
        name="matmul",
    )(a, b)


def _out_proj_kernel(a_ref, b_ref, c_ref, w_ref, o_ref):
    o_ref[...] = (jnp.dot(a_ref[...], w_ref[:A_WIDTH, :], preferred_element_type=F32)
                  + jnp.dot(b_ref[...], w_ref[A_WIDTH:A_WIDTH + B_WIDTH, :], preferred_element_type=F32)
                  + jnp.dot(c_ref[...], w_ref[A_WIDTH + B_WIDTH:, :], preferred_element_type=F32))


def out_proj(oa, ob, oc, w, tm, tn):
    m = oa.shape[0]
    k, n = w.shape
    tm = min(tm, m)
    return pl.pallas_call(
        _out_proj_kernel,
        grid=(n // tn, m // tm),
        in_specs=[pl.BlockSpec((tm, A_WIDTH), lambda j, i: (i, 0)),
                  pl.BlockSpec((tm, B_WIDTH), lambda j, i: (i, 0)),
                  pl.BlockSpec((tm, C_WIDTH), lambda j, i: (i, 0)),
                  pl.BlockSpec((k, tn), lambda j, i: (0, j))],
        out_specs=pl.BlockSpec((tm, tn), lambda j, i: (i, j)),
        out_shape=jax.ShapeDtypeStruct((m, n), F32),
        compiler_params=_cparams(("parallel", "parallel")),
        name="out_proj",
    )(oa, ob, oc, w)


def _hgrn_kernel(q_ref, f_ref, i_ref, g_ref, lb_ref, ng_ref, o_ref,
                 state_ref, cum_ref, kk_ref, acc_ref, *, n_chunks):
    @pl.when(pl.program_id(1) == 0)
    def _():
        state_ref[...] = jnp.zeros_like(state_ref)

    lb = lb_ref[...]
    ng = ng_ref[...]
    row = lax.broadcasted_iota(jnp.int32, (CHUNK, CHUNK), 0)
    col = lax.broadcasted_iota(jnp.int32, (CHUNK, CHUNK), 1)
    tri = (col <= row).astype(F32)
    sub = 8

    for c in range(n_chunks):
        r0 = c * CHUNK
        f = lb + (1.0 - lb) * jax.nn.sigmoid(f_ref[r0:r0 + CHUNK, :])
        kk = 1.0 - f
        cum = jnp.dot(tri, jnp.log(f), precision=HI, preferred_element_type=F32)
        qs = q_ref[r0:r0 + CHUNK, :] * (A_KEY ** -0.5)
        inter = lax.dot_general(qs * jnp.exp(cum), state_ref[...], (((1,), (1,)), ((), ())),
                                precision=HI, preferred_element_type=F32)
        cum_ref[...] = cum
        kk_ref[...] = kk
        acc_ref[...] = inter

        v = i_ref[r0:r0 + CHUNK, :]

        for sb in range(CHUNK // sub):
            t0 = sb * sub
            t1 = t0 + sub
            qs_d, cum_d = qs[t0:t1, :], cum[t0:t1, :]
            qs_b, cum_b = qs[t1:, :], cum[t1:, :]
            t_idx = t0 + lax.broadcasted_iota(jnp.int32, (sub, 1), 0)

            below = CHUNK > t1

            def body(s, acc, qs_d=qs_d, cum_d=cum_d, qs_b=qs_b, cum_b=cum_b, t_idx=t_idx, r0=r0, below=below):
                rc = cum_ref[pl.ds(s, 1), :]
                rk = kk_ref[pl.ds(s, 1), :]
                rv = i_ref[pl.ds(r0 + s, 1), :]
                a = jnp.sum(qs_d * rk * jnp.exp(jnp.minimum(cum_d - rc, 0.0)), axis=1, keepdims=True)
                acc_d = acc[0] + jnp.where(t_idx >= s, a, 0.0) * rv
                if not below:
                    return (acc_d,)
                return acc_d, acc[1] + jnp.sum(qs_b * rk * jnp.exp(cum_b - rc), axis=1, keepdims=True) * rv

            init = (jnp.zeros((sub, A_VAL), F32),) + ((jnp.zeros((CHUNK - t1, A_VAL), F32),) if below else ())
            acc = lax.fori_loop(t0, t1, body, init, unroll=True)
            acc_ref[t0:t1, :] += acc[0]
            if below:
                acc_ref[t1:, :] += acc[1]

        last = cum[CHUNK - 1:CHUNK, :]
        kd = kk * jnp.exp(last - cum)
        state_ref[...] = state_ref[...] * jnp.exp(last) + lax.dot_general(
            v, kd, (((0,), (0,)), ((), ())), precision=HI, preferred_element_type=F32)

        o = acc_ref[...]
        o = o * lax.rsqrt(jnp.mean(o * o, axis=-1, keepdims=True) + NORM_EPS) * ng
        g = g_ref[r0:r0 + CHUNK, :]
        o_ref[r0:r0 + CHUNK, :] = (o * (g * jax.nn.sigmoid(g))).astype(o_ref.dtype)


def hgrn_mixer(p, lb, ng, rows=256):
    t = p.shape[0]
    rows = min(rows, t)

    def seg(off):
        return pl.BlockSpec((rows, LANES), lambda h, r: (r, off + h))

    vec = pl.BlockSpec((1, LANES), lambda h, r: (0, h))
    return pl.pallas_call(
        functools.partial(_hgrn_kernel, n_chunks=rows // CHUNK),
        grid=(A_HEADS, t // rows),
        in_specs=[seg(OFF_AQ), seg(OFF_AF), seg(OFF_AI), seg(OFF_AG), vec, vec],
        out_specs=pl.BlockSpec((rows, LANES), lambda h, r: (r, h)),
        out_shape=jax.ShapeDtypeStruct((t, A_WIDTH), BF16),
        scratch_shapes=[pltpu.VMEM((A_VAL, A_KEY), F32), pltpu.VMEM((CHUNK, A_KEY), F32),
                        pltpu.VMEM((CHUNK, A_KEY), F32), pltpu.VMEM((CHUNK, A_VAL), F32)],
        compiler_params=_cparams(("parallel", "arbitrary")),
        name="hgrn_mixer",
    )(p, p, p, p, lb, ng)


def _ret_consts():
    lg = np.log(1.0 - np.exp(np.linspace(np.log(1.0 / 32), np.log(1.0 / 512), C_HEADS)))
    pos = np.arange(CHUNK, dtype=np.float64)
    rel = pos[:, None] - pos[None, :]
    dmat = np.where(rel >= 0, np.exp(lg[:, None, None] * np.maximum(rel, 0.0)), 0.0)
    q_dec = np.broadcast_to(np.exp(lg[:, None] * (pos + 1.0))[..., None], (C_HEADS, CHUNK, C_KEY))
    k_dec = np.broadcast_to(np.exp(lg[:, None] * (CHUNK - 1.0 - pos))[..., None], (C_HEADS, CHUNK, C_KEY))
    c_dec = np.broadcast_to(np.exp(lg * CHUNK)[:, None, None], (C_HEADS, 1, C_VAL))
    return tuple(jnp.asarray(np.ascontiguousarray(a), F32) for a in (dmat, q_dec, k_dec, c_dec))


def _ret_kernel(q_ref, k_ref, v_ref, g_ref, cos_ref, sin_ref, dm_ref, qd_ref, kd_ref, cd_ref, ng_ref,
                o_ref, state_ref, *, n_chunks):
    @pl.when(pl.program_id(1) == 0)
    def _():
        state_ref[...] = jnp.zeros_like(state_ref)

    dmat = dm_ref[...]
    q_dec = qd_ref[...]
    k_dec = kd_ref[...]
    c_dec = cd_ref[...]
    ng = ng_ref[...]
    for c in range(n_chunks):
        r0 = c * CHUNK
        cos = cos_ref[r0:r0 + CHUNK, :]
        sin = sin_ref[r0:r0 + CHUNK, :]
        q = q_ref[r0:r0 + CHUNK, :]
        k = k_ref[r0:r0 + CHUNK, :]
        q = q * cos + pltpu.roll(q, C_KEY // 2, 1) * sin
        k = (k * cos + pltpu.roll(k, C_KEY // 2, 1) * sin) * (C_KEY ** -0.5)
        v = v_ref[r0:r0 + CHUNK, :].astype(BF16)
        s = lax.dot_general(q.astype(BF16), k.astype(BF16), (((1,), (1,)), ((), ())), preferred_element_type=F32)
        intra = jnp.dot((s * dmat).astype(BF16), v, preferred_element_type=F32)
        inter = jnp.dot((q * q_dec).astype(BF16), state_ref[...].astype(BF16), preferred_element_type=F32)
        state_ref[...] = c_dec * state_ref[...] + jnp.dot(
            jnp.transpose(k * k_dec).astype(BF16), v, preferred_element_type=F32)
        o = intra + inter
        mu = jnp.mean(o, axis=-1, keepdims=True)
        oc = o - mu
        var = jnp.mean(oc * oc, axis=-1, keepdims=True)
        o = oc * lax.rsqrt(var + NORM_EPS) * ng
        g = g_ref[r0:r0 + CHUNK, :]
        o_ref[r0:r0 + CHUNK, :] = (o * (g * jax.nn.sigmoid(g))).astype(o_ref.dtype)


def retention_mixer(p, cos, sin, ng, rows=256):
    t = p.shape[0]
    rows = min(rows, t)
    dmat, q_dec, k_dec, c_dec = _ret_consts()
    tab = pl.BlockSpec((rows, C_KEY), lambda h, r: (r, 0))
    return pl.pallas_call(
        functools.partial(_ret_kernel, n_chunks=rows // CHUNK),
        grid=(C_HEADS, t // rows),
        in_specs=[pl.BlockSpec((rows, C_KEY), lambda h, r: (r, OFF_CQ + h)),
                  pl.BlockSpec((rows, C_KEY), lambda h, r: (r, OFF_CK + h)),
                  pl.BlockSpec((rows, C_VAL), lambda h, r: (r, OFF_CV // 2 + h)),
                  pl.BlockSpec((rows, C_VAL), lambda h, r: (r, OFF_CG // 2 + h)),
                  tab, tab,
                  pl.BlockSpec((None, CHUNK, CHUNK), lambda h, r: (h, 0, 0)),
                  pl.BlockSpec((None, CHUNK, C_KEY), lambda h, r: (h, 0, 0)),
                  pl.BlockSpec((None, CHUNK, C_KEY), lambda h, r: (h, 0, 0)),
                  pl.BlockSpec((None, 1, C_VAL), lambda h, r: (h, 0, 0)),
                  pl.BlockSpec((1, C_VAL), lambda h, r: (0, h))],
        out_specs=pl.BlockSpec((rows, C_VAL), lambda h, r: (r, h)),
        out_shape=jax.ShapeDtypeStruct((t, C_WIDTH), BF16),
        scratch_shapes=[pltpu.VMEM((C_KEY, C_VAL), F32)],
        compiler_params=_cparams(("parallel", "arbitrary")),
        name="retention_mixer",
    )(p, p, p, p, cos, sin, dmat, q_dec, k_dec, c_dec, ng)


def _rope_tables(t, n_rot, theta, width):
    half = n_rot // 2
    inv_freq = 1.0 / (theta ** (jnp.arange(0, n_rot, 2, dtype=F32) / n_rot))
    ang = jnp.arange(t, dtype=F32)[:, None] * inv_freq[None, :]
    c, s = jnp.cos(ang), jnp.sin(ang)
    z = jnp.zeros((t, width - n_rot), F32)
    zh = jnp.zeros((t, half), F32)
    cos = jnp.concatenate([c, c, jnp.ones((t, width - n_rot), F32)], axis=1)
    sin_lo = jnp.concatenate([-s, zh, z], axis=1)
    sin_hi = jnp.concatenate([zh, s, z], axis=1)
    rep = LANES // width
    return tuple(jnp.tile(a, (1, rep)) for a in (cos, sin_lo, sin_hi))


def _rope128(x, cos, sin_lo, sin_hi, half):
    return x * cos + pltpu.roll(x, LANES - half, 1) * sin_lo + pltpu.roll(x, half, 1) * sin_hi


def _dsa_prep_kernel(q_ref, k_ref, v_ref, iq_ref, ikw_ref, cb_ref, slb_ref, shb_ref, ci_ref, sli_ref, shi_ref,
                     qo_ref, ko_ref, vo_ref, iqo_ref, iko_ref, wo_ref):
    cb, slb, shb = cb_ref[...], slb_ref[...], shb_ref[...]
    ci, sli, shi = ci_ref[...], sli_ref[...], shi_ref[...]
    for h in range(B_HEADS):
        sl = slice(h * LANES, (h + 1) * LANES)
        qo_ref[:, sl] = _rope128(q_ref[:, sl], cb, slb, shb, B_ROT // 2).astype(BF16)
        ko_ref[:, sl] = _rope128(k_ref[:, sl], cb, slb, shb, B_ROT // 2).astype(BF16)
        iqo_ref[:, sl] = _rope128(iq_ref[:, sl], ci, sli, shi, IDX_ROT // 2).astype(BF16)
    vo_ref[...] = v_ref[...].astype(BF16)
    ikw = ikw_ref[...]
    iko_ref[...] = _rope128(ikw, ci, sli, shi, IDX_ROT // 2)[:, :IDX_DIM].astype(BF16)
    wo_ref[...] = ikw[:, IDX_DIM:IDX_DIM + IDX_HEADS] * (IDX_DIM ** -0.5 * IDX_HEADS ** -0.5)


def dsa_prep(p, rows=256):
    t = p.shape[0]
    rows = min(rows, t)
    tabs_b = _rope_tables(t, B_ROT, ROPE_THETA, B_HEAD_DIM)
    tabs_i = _rope_tables(t, IDX_ROT, ROPE_THETA, IDX_DIM)
    wide = lambda off: pl.BlockSpec((rows, B_WIDTH), lambda r: (r, off // 8))
    tab = pl.BlockSpec((rows, LANES), lambda r: (r, 0))
    out_wide = pl.BlockSpec((rows, B_WIDTH), lambda r: (r, 0))
    return pl.pallas_call(
        _dsa_prep_kernel,
        grid=(t // rows,),
        in_specs=[wide(OFF_BQ), wide(OFF_BK), wide(OFF_BV), wide(OFF_IQ),
                  pl.BlockSpec((rows, LANES), lambda r: (r, OFF_IKW))] + [tab] * 6,
        out_specs=[out_wide, out_wide, out_wide, out_wide,
                   pl.BlockSpec((rows, IDX_DIM), lambda r: (r, 0)),
                   pl.BlockSpec((rows, IDX_HEADS), lambda r: (r, 0))],
        out_shape=[jax.ShapeDtypeStruct((t, B_WIDTH), BF16)] * 4
        + [jax.ShapeDtypeStruct((t, IDX_DIM), BF16), jax.ShapeDtypeStruct((t, IDX_HEADS), F32)],
        compiler_params=_cparams(("parallel",)),
        name="dsa_prep",
    )(p, p, p, p, p, *tabs_b, *tabs_i)


def _dsa_kernel(q_ref, qi_ref, w_ref, kit_ref, k_ref, v_ref, o_ref,
                keys_ref, thr_ref, tie_ref, m_ref, l_ref, acc_ref, *, qb, kb, topk, pos_bits):
    i = pl.program_id(0)
    j = pl.program_id(1)
    nkb = ((i + 1) * qb + kb - 1) // kb

    @pl.when(j == 0)
    def _():
        q_pos = i * qb + lax.broadcasted_iota(jnp.int32, (qb, 1), 0)
        limit = (q_pos // CHUNK + 1) * CHUNK
        w = w_ref[...]

        def score_block(b, carry):
            off = pl.multiple_of(b * kb, kb)
            kit = kit_ref[:, pl.ds(off, kb)]
            sc = jnp.zeros((qb, kb), F32)
            for h in range(IDX_HEADS):
                lg = jnp.dot(qi_ref[:, h * IDX_DIM:(h + 1) * IDX_DIM], kit, preferred_element_type=F32)
                sc = sc + jnp.maximum(lg, 0.0) * w[:, h:h + 1]
            bits = pltpu.bitcast(sc, jnp.int32)
            key = jnp.where(bits < 0, bits ^ jnp.int32(0x7FFFFFFF), bits)
            k_pos = off + lax.broadcasted_iota(jnp.int32, (1, kb), 1)
            keys_ref[:, pl.ds(off, kb)] = jnp.where(k_pos < limit, key, jnp.int32(INT_MIN))
            return carry

        lax.fori_loop(0, nkb, score_block, 0)

        def count_rows(make_hit):
            cnts = []
            for g in range(qb // COUNT_ROWS):
                rows = slice(g * COUNT_ROWS, (g + 1) * COUNT_ROWS)
                hit = make_hit(rows)

                def count_block(b, cnt, rows=rows, hit=hit):
                    off = pl.multiple_of(b * kb, kb)
                    for c in range(kb // LANES):
                        blk = keys_ref[rows, pl.ds(off + c * LANES, LANES)]
                        cnt = cnt + hit(blk, off + c * LANES).astype(jnp.int32)
                    return cnt

                cnts.append(lax.fori_loop(0, nkb, count_block, jnp.zeros((COUNT_ROWS, LANES), jnp.int32)))
            return jnp.sum(jnp.concatenate(cnts, axis=0), axis=1, keepdims=True)

        def count_ge(cand):
            cand_b = jnp.broadcast_to(cand, (qb, LANES))
            return count_rows(lambda rows: (lambda blk, pos, c=cand_b[rows]: blk >= c))

        def bit_step(it, thr):
            cand = thr ^ lax.shift_left(jnp.int32(1), 31 - it)
            return jnp.where(count_ge(cand) >= topk, cand, thr)

        thr = lax.fori_loop(0, 32, bit_step, jnp.full((qb, 1), INT_MIN, jnp.int32))
        thr = jnp.maximum(thr, jnp.int32(INT_MIN + 1))
        thr_ref[...] = thr

        n_ge = count_ge(thr)
        tie_ref[...] = jnp.full_like(tie_ref, 2 ** 31 - 1)

        @pl.when(jnp.max(n_ge) > topk)
        def _():
            need = topk - count_ge(thr + 1)
            thr_b = jnp.broadcast_to(thr, (qb, LANES))
            lane = lax.broadcasted_iota(jnp.int32, (COUNT_ROWS, LANES), 1)

            def pos_step(it, x):
                cand = x | lax.shift_left(jnp.int32(1), pos_bits - 1 - it)
                cand_b = jnp.broadcast_to(cand, (qb, LANES))
                tied_before = count_rows(
                    lambda rows: (lambda blk, pos, th=thr_b[rows], c=cand_b[rows]: (blk == th) & (lane < c - pos)))
                return jnp.where(tied_before < need, cand, x)

            x = lax.fori_loop(0, pos_bits, pos_step, jnp.zeros((qb, 1), jnp.int32))
            tie_ref[...] = jnp.where(n_ge > topk, x, 2 ** 31 - 1)

        m_ref[...] = jnp.full_like(m_ref, -1e30)
        l_ref[...] = jnp.zeros_like(l_ref)
        acc_ref[...] = jnp.zeros_like(acc_ref)

    @pl.when(j < nkb)
    def _():
        off = pl.multiple_of(j * kb, kb)
        key = keys_ref[:, pl.ds(off, kb)]
        thr = thr_ref[...]
        k_pos = off + lax.broadcasted_iota(jnp.int32, (1, kb), 1)
        mask = (key >= thr) & ((key > thr) | (k_pos <= tie_ref[...]))
        for h in range(B_HEADS):
            sl = slice(h * B_HEAD_DIM, (h + 1) * B_HEAD_DIM)
            s = lax.dot_general(q_ref[:, sl], k_ref[:, sl], (((1,), (1,)), ((), ())),
                                preferred_element_type=F32) * (B_HEAD_DIM ** -0.5)
            s = jnp.where(mask, s, -1e30)
            m_old = m_ref[h]
            m_new = jnp.maximum(m_old, jnp.max(s, axis=1, keepdims=True))
            alpha = jnp.exp(m_old - m_new)
            p = jnp.exp(s - m_new)
            l_ref[h] = alpha * l_ref[h] + jnp.sum(p, axis=1, keepdims=True)
            acc_ref[:, sl] = alpha * acc_ref[:, sl] + jnp.dot(p.astype(BF16), v_ref[:, sl],
                                                              preferred_element_type=F32)
            m_ref[h] = m_new

    @pl.when(j == nkb - 1)
    def _():
        for h in range(B_HEADS):
            sl = slice(h * B_HEAD_DIM, (h + 1) * B_HEAD_DIM)
            o_ref[:, sl] = (acc_ref[:, sl] / l_ref[h]).astype(o_ref.dtype)


def dsa_mixer(p, qb=256, kb=1024):
    t = p.shape[0]
    qb = min(qb, t)
    kb = min(kb, t)
    topk = min(IDX_TOPK_MAX, t // 4)
    q, k, v, qi, ki, wi = dsa_prep(p)
    kit = ki.T

    def kv_map(i, j):
        return (jnp.minimum(j, ((i + 1) * qb + kb - 1) // kb - 1), 0)

    q_spec = pl.BlockSpec((qb, B_WIDTH), lambda i, j: (i, 0))
    return pl.pallas_call(
        functools.partial(_dsa_kernel, qb=qb, kb=kb, topk=topk, pos_bits=max(1, (t - 1).bit_length())),
        grid=(t // qb, t // kb),
        in_specs=[q_spec, q_spec,
                  pl.BlockSpec((qb, IDX_HEADS), lambda i, j: (i, 0)),
                  pl.BlockSpec((IDX_DIM, t), lambda i, j: (0, 0)),
                  pl.BlockSpec((kb, B_WIDTH), kv_map),
                  pl.BlockSpec((kb, B_WIDTH), kv_map)],
        out_specs=q_spec,
        out_shape=jax.ShapeDtypeStruct((t, B_WIDTH), BF16),
        scratch_shapes=[pltpu.VMEM((qb, t), jnp.int32), pltpu.VMEM((qb, 1), jnp.int32),
                        pltpu.VMEM((qb, 1), jnp.int32),
                        pltpu.VMEM((B_HEADS, qb, 1), F32), pltpu.VMEM((B_HEADS, qb, 1), F32),
                        pltpu.VMEM((qb, B_WIDTH), F32)],
        compiler_params=_cparams(("arbitrary", "arbitrary")),
        name="dsa_mixer",
    )(q, qi, wi, kit, k, v)


EXPERT_F_TILES = 3
DOWN_COLS = 1024
EXPERT_VMEM_LIMIT = 60 * 1024 * 1024


def _expert_kernel(be_ref, nu_ref, first_ref, nv_ref, tok_ref, h_hbm, wg_ref, wu_ref, bg_ref, bu_ref, wd_ref,
                   bd_ref, o_ref, xbuf_ref, xb_ref, act_ref, sem):
    b = pl.program_id(0)
    f = pl.program_id(1)
    n_used = nu_ref[0]
    used = b < n_used

    def row_copy(r, src_row):
        return pltpu.make_async_copy(h_hbm.at[pl.ds(src_row, 1), :], xbuf_ref.at[pl.ds(r, 1), :], sem)

    def issue_block(blk):
        first = first_ref[blk]
        last_valid = nv_ref[blk] - 1

        def issue(r, carry):
            row_copy(r, tok_ref[first + jnp.minimum(r, last_valid)]).start()
            return carry
        lax.fori_loop(0, MOE_BLOCK, issue, 0, unroll=8)

    @pl.when(jnp.logical_not(used) & (f == 0))
    def _():
        o_ref[...] = jnp.zeros_like(o_ref)

    @pl.when(used & (f == 0))
    def _():
        @pl.when(b == 0)
        def _():
            issue_block(0)

        def drain(r, carry):
            row_copy(r, 0).wait()
            return carry
        lax.fori_loop(0, MOE_BLOCK, drain, 0, unroll=8)
        xb_ref[...] = xbuf_ref[...].astype(BF16)

        @pl.when(b + 1 < n_used)
        def _():
            issue_block(b + 1)

    @pl.when(used)
    def _():
        x = xb_ref[...]
        g_lin = jnp.dot(x, wg_ref[...], preferred_element_type=F32) + bg_ref[...]
        u_lin = jnp.dot(x, wu_ref[...], preferred_element_type=F32) + bu_ref[...]
        g_lin = jnp.minimum(g_lin, SWIGLU_LIMIT)
        u_lin = jnp.clip(u_lin, -SWIGLU_LIMIT, SWIGLU_LIMIT)
        act = (u_lin + 1.0) * g_lin * jax.nn.sigmoid(SWIGLU_ALPHA * g_lin)
        ft = D_EXPERT // EXPERT_F_TILES
        tile = jnp.where(b % 2 == 0, f, EXPERT_F_TILES - 1 - f)
        act_ref[:, pl.ds(pl.multiple_of(tile * ft, ft), ft)] = act.astype(BF16)

        @pl.when(f == EXPERT_F_TILES - 1)
        def _():
            a = act_ref[...]
            for c in range(0, D_MODEL, DOWN_COLS):
                cols = slice(c, c + DOWN_COLS)
                o_ref[:, cols] = jnp.dot(a, wd_ref[:, cols], preferred_element_type=F32) + bd_ref[:, cols]


def expert_ffn(h, tok_sorted, block_expert, first, n_valid, n_used, w_gu, b_gu, w_d, b_d):
    d = h.shape[1]
    n_blocks = block_expert.shape[0]
    ft = D_EXPERT // EXPERT_F_TILES

    def expert(b, be, nu):
        return be[jnp.minimum(b, nu[0] - 1)]

    def tile(b, f, nu):
        f = jnp.where(b < nu[0], f, EXPERT_F_TILES - 1)
        return jnp.where(jnp.minimum(b, nu[0] - 1) % 2 == 0, f, EXPERT_F_TILES - 1 - f)

    return pl.pallas_call(
        _expert_kernel,
        grid_spec=pltpu.PrefetchScalarGridSpec(
            num_scalar_prefetch=5,
            grid=(n_blocks, EXPERT_F_TILES),
            in_specs=[pl.BlockSpec(memory_space=pl.ANY),
                      pl.BlockSpec((None, d, ft), lambda b, f, be, nu, *_: (expert(b, be, nu), 0, tile(b, f, nu))),
                      pl.BlockSpec((None, d, ft),
                                   lambda b, f, be, nu, *_: (expert(b, be, nu), 0, EXPERT_F_TILES + tile(b, f, nu))),
                      pl.BlockSpec((None, 1, ft), lambda b, f, be, nu, *_: (expert(b, be, nu), 0, tile(b, f, nu))),
                      pl.BlockSpec((None, 1, ft),
                                   lambda b, f, be, nu, *_: (expert(b, be, nu), 0, EXPERT_F_TILES + tile(b, f, nu))),
                      pl.BlockSpec((None, D_EXPERT, d), lambda b, f, be, nu, *_: (expert(b, be, nu), 0, 0)),
                      pl.BlockSpec((None, 1, d), lambda b, f, be, nu, *_: (expert(b, be, nu), 0, 0))],
            out_specs=pl.BlockSpec((MOE_BLOCK, d), lambda b, f, *_: (b, 0)),
            scratch_shapes=[pltpu.VMEM((MOE_BLOCK, d), F32), pltpu.VMEM((MOE_BLOCK, d), BF16),
                            pltpu.VMEM((MOE_BLOCK, D_EXPERT), BF16), pltpu.SemaphoreType.DMA(())]),
        out_shape=jax.ShapeDtypeStruct((n_blocks * MOE_BLOCK, d), F32),
        compiler_params=_cparams(("arbitrary", "arbitrary"), EXPERT_VMEM_LIMIT),
        name="moe_experts",
    )(block_expert, n_used, first, n_valid, tok_sorted, h, w_gu, w_gu, b_gu, b_gu, w_d, b_d)


def _combine_ln_kernel(pos_ref, ys_hbm, gate_ref, h_ref, g_ref, b_ref, o_ref, ob_ref, buf_ref, sem, *, rows):
    i = pl.program_id(0)

    def row_copy(tile, r, k, src_row):
        slot = tile % 2
        return pltpu.make_async_copy(ys_hbm.at[pl.ds(src_row, 1), :], buf_ref.at[slot, k, pl.ds(r, 1), :],
                                     sem.at[slot])

    def issue_tile(tile):
        def issue(r, carry):
            for k in range(TOP_K):
                row_copy(tile, r, k, pos_ref[(tile * rows + r) * TOP_K + k]).start()
            return carry
        lax.fori_loop(0, rows, issue, 0, unroll=4)

    @pl.when(i == 0)
    def _():
        issue_tile(0)

    @pl.when(i + 1 < pl.num_programs(0))
    def _():
        issue_tile(i + 1)

    def drain(r, carry):
        for k in range(TOP_K):
            row_copy(i, r, k, 0).wait()
        return carry

    lax.fori_loop(0, rows, drain, 0)
    slot = i % 2
    gate = gate_ref[...]
    ffn = ((buf_ref[slot, 0] * gate[:, 0:1] + buf_ref[slot, 1] * gate[:, 1:2])
           + (buf_ref[slot, 2] * gate[:, 2:3] + buf_ref[slot, 3] * gate[:, 3:4]))
    y = _ln_rows(DN_ALPHA * h_ref[...] + ffn, g_ref[...], b_ref[...])
    o_ref[...] = y
    ob_ref[...] = y.astype(BF16)


def combine_ln(ys, pos, gates, h, g, b, rows=64):
    t, d = h.shape
    rows = min(rows, t)
    row_spec = pl.BlockSpec((rows, d), lambda i, pos: (i, 0))
    vec_spec = pl.BlockSpec((1, d), lambda i, pos: (0, 0))
    return pl.pallas_call(
        functools.partial(_combine_ln_kernel, rows=rows),
        grid_spec=pltpu.PrefetchScalarGridSpec(
            num_scalar_prefetch=1,
            grid=(t // rows,),
            in_specs=[pl.BlockSpec(memory_space=pl.ANY), pl.BlockSpec((rows, TOP_K), lambda i, pos: (i, 0)),
                      row_spec, vec_spec, vec_spec],
            out_specs=[row_spec, row_spec],
            scratch_shapes=[pltpu.VMEM((2, TOP_K, rows, d), F32), pltpu.SemaphoreType.DMA((2,))]),
        out_shape=[jax.ShapeDtypeStruct((t, d), F32), jax.ShapeDtypeStruct((t, d), BF16)],
        compiler_params=_cparams(("arbitrary",)),
        name="moe_combine_ln",
    )(pos, ys, gates, h, g.reshape(1, d), b.reshape(1, d))


def moe_routing(logits):
    n_tok = logits.shape[0]
    n_assign = n_tok * TOP_K
    n_blocks = -(-n_assign // MOE_BLOCK) + N_EXPERTS
    top_val, top_idx = lax.top_k(logits, TOP_K)
    gates = jax.nn.softmax(top_val, axis=-1)
    flat_e = top_idx.reshape(-1).astype(jnp.int32)
    iota = jnp.arange(n_assign, dtype=jnp.int32)
    se, order = lax.sort((flat_e, iota), num_keys=1, is_stable=True)
    experts = jnp.arange(N_EXPERTS, dtype=jnp.int32)
    counts = jnp.sum((flat_e[:, None] == experts[None, :]).astype(jnp.int32), axis=0)
    padded = (counts + MOE_BLOCK - 1) // MOE_BLOCK * MOE_BLOCK
    start = jnp.cumsum(counts) - counts
    pend = jnp.cumsum(padded)
    shift = pend - padded - start
    dest = iota + jnp.sum(jnp.where(se[:, None] == experts[None, :], shift[None, :], 0), axis=1)
    _, pos = lax.sort((order, dest), num_keys=1)
    blk = jnp.arange(n_blocks, dtype=jnp.int32)
    block_expert = jnp.minimum(
        jnp.sum((blk[:, None] * MOE_BLOCK >= pend[None, :]).astype(jnp.int32), axis=1), N_EXPERTS - 1)
    first = blk * MOE_BLOCK - shift[block_expert]
    n_valid = jnp.clip(start[block_expert] + counts[block_expert] - first, 0, MOE_BLOCK)
    n_used = (pend[-1:] // MOE_BLOCK).astype(jnp.int32)
    return order // TOP_K, gates, pos, block_expert, first, n_valid, n_used


def _pack_w_kernel(w_ref, o_ref):
    n_tail = w_ref.shape[1] - IKW_END
    ikw_at = IKW_START + n_tail
    o_ref[:, :IKW_START] = w_ref[:, :IKW_START].astype(BF16)
    o_ref[:, IKW_START:ikw_at] = w_ref[:, IKW_END:].astype(BF16)
    o_ref[:, ikw_at:ikw_at + IKW_END - IKW_START] = w_ref[:, IKW_START:IKW_END].astype(BF16)
    o_ref[:, ikw_at + IKW_END - IKW_START:] = jnp.zeros((o_ref.shape[0], PROJ_PAD - w_ref.shape[1]), BF16)


def pack_w_in(w_in, layer, rows=128):
    _, d, n = w_in.shape
    assert IKW_START + (n - IKW_END) == OFF_IKW * LANES
    return pl.pallas_call(
        _pack_w_kernel,
        grid=(d // rows,),
        in_specs=[pl.BlockSpec((None, rows, n), lambda i: (layer, i, 0))],
        out_specs=pl.BlockSpec((rows, PROJ_PAD), lambda i: (i, 0)),
        out_shape=jax.ShapeDtypeStruct((d, PROJ_PAD), BF16),
        compiler_params=_cparams(("parallel",)),
        name="pack_w_in",
    )(w_in)


def kernel(x, ln_in_g, ln_in_b, w_in, w_out, hgrn_lb, hgrn_norm_g, ret_norm_g, ln1_g, ln1_b,
           router_w, router_b, w_gate_up, b_gate_up, w_down, b_down, ln2_g, ln2_b):
    bsz, seq, d = x.shape
    assert bsz == 1
    t = seq
    lb_all = jnp.cumsum(jax.nn.softmax(hgrn_lb.astype(F32), axis=0), axis=0)
    lb_all = lb_all - lb_all[0:1]
    ang = jnp.arange(t, dtype=F32)[:, None] * (
        1.0 / (RET_THETA ** (jnp.arange(0, C_KEY, 2, dtype=F32) / C_KEY)))[None, :]
    cos_r = jnp.concatenate([jnp.cos(ang), jnp.cos(ang)], axis=1)
    sin_r = jnp.concatenate([-jnp.sin(ang), jnp.sin(ang)], axis=1)

    h, hb = ln_in(x.reshape(t, d), ln_in_g, ln_in_b)
    for l in range(DEPTH):
        p = matmul(hb, pack_w_in(w_in, l), tm=PROJ_ROWS, tn=9 * LANES)
        oa = hgrn_mixer(p, lb_all[l].reshape(1, A_WIDTH), hgrn_norm_g[l].reshape(1, A_WIDTH))
        ob = dsa_mixer(p)
        oc = retention_mixer(p, cos_r, sin_r, ret_norm_g[l].reshape(1, C_WIDTH))
        mm = out_proj(oa, ob, oc, w_out[l].astype(BF16), tm=PROJ_ROWS, tn=1024)
        rw = jnp.pad(router_w[l], ((0, 0), (0, LANES - N_EXPERTS))).astype(BF16)
        rb = jnp.pad(router_b[l], (0, LANES - N_EXPERTS)).reshape(1, LANES)
        h1, logits = ln1_router(h, mm, ln1_g[l], ln1_b[l], rw, rb)
        tok_sorted, gates, pos, block_expert, first, n_valid, n_used = moe_routing(logits[:, :N_EXPERTS])
        ys = expert_ffn(h1, tok_sorted, block_expert, first, n_valid, n_used,
                        w_gate_up[l].astype(BF16), b_gate_up[l].reshape(N_EXPERTS, 1, 2 * D_EXPERT),
                        w_down[l].astype(BF16), b_down[l].reshape(N_EXPERTS, 1, d))
        h, hb = combine_ln(ys, pos, gates, h1, ln2_g[l], ln2_b[l])
    return h.reshape(bsz, t, d)
```

```python
import functools
import math

import numpy as np
import jax
import jax.numpy as jnp
from jax import lax
from jax.experimental import pallas as pl
from jax.experimental.pallas import tpu as pltpu

F32 = jnp.float32
BF16 = jnp.bfloat16
HI = lax.Precision.HIGHEST

LANES = 128
VMEM_LIMIT = 56 * 1024 * 1024

D_MODEL = 4096
DEPTH = 2
CHUNK = 64

A_KEY = 128
A_VAL = 128
A_WIDTH = 1536
A_HEADS = 12

B_WIDTH = 1024
B_HEAD_DIM = 128
B_HEADS = 8
B_ROT = 32
IDX_HEADS = 16
IDX_DIM = 64
IDX_ROT = 16
IDX_TOPK_MAX = 256
ROPE_THETA = 500000.0

C_WIDTH = 1536
C_VAL = 256
C_KEY = 128
C_HEADS = 6
RET_THETA = 10000.0

N_EXPERTS = 32
TOP_K = 4
D_EXPERT = 768
SWIGLU_LIMIT = 7.0
SWIGLU_ALPHA = 1.702
MOE_BLOCK = 512

DN_ALPHA = (2 * DEPTH) ** 0.25
LN_EPS = 1e-5
NORM_EPS = 1e-6

OFF_AQ, OFF_AF, OFF_AI, OFF_AG = 0, 12, 24, 36
OFF_BQ, OFF_BK, OFF_BV, OFF_IQ = 48, 56, 64, 72
OFF_CQ, OFF_CK, OFF_CV, OFF_CG, OFF_IKW = 80, 86, 92, 104, 116
PROJ_BLOCKS = 117
PROJ_PAD = PROJ_BLOCKS * LANES
IKW_START, IKW_END = 10240, 10320

INT_MIN = -2 ** 31
PROJ_ROWS = 1024
COUNT_ROWS = 128


def _cparams(sem, vmem=VMEM_LIMIT):
    return pltpu.CompilerParams(dimension_semantics=sem, vmem_limit_bytes=vmem)


def _ln_rows(x, g, b):
    mu = jnp.mean(x, axis=-1, keepdims=True)
    xc = x - mu
    var = jnp.mean(xc * xc, axis=-1, keepdims=True)
    return xc * lax.rsqrt(var + LN_EPS) * g + b


def _ln_in_kernel(x_ref, g_ref, b_ref, o_ref, ob_ref):
    y = _ln_rows(x_ref[...], g_ref[...], b_ref[...])
    o_ref[...] = y
    ob_ref[...] = y.astype(BF16)


def ln_in(x, g, b, rows=256):
    t, d = x.shape
    rows = min(rows, t)
    row_spec = pl.BlockSpec((rows, d), lambda i: (i, 0))
    vec_spec = pl.BlockSpec((1, d), lambda i: (0, 0))
    return pl.pallas_call(
        _ln_in_kernel,
        grid=(t // rows,),
        in_specs=[row_spec, vec_spec, vec_spec],
        out_specs=[row_spec, row_spec],
        out_shape=[jax.ShapeDtypeStruct((t, d), F32), jax.ShapeDtypeStruct((t, d), BF16)],
        compiler_params=_cparams(("parallel",)),
        name="ln_in",
    )(x, g.reshape(1, d), b.reshape(1, d))


def _ln1_router_kernel(h_ref, mm_ref, g_ref, b_ref, rw_ref, rb_ref, o_ref, lg_ref):
    y = _ln_rows(DN_ALPHA * h_ref[...] + mm_ref[...], g_ref[...], b_ref[...])
    o_ref[...] = y
    lg_ref[...] = jnp.dot(y.astype(BF16), rw_ref[...], preferred_element_type=F32) + rb_ref[...]


def ln1_router(h, mm, g, b, rw, rb, rows=256):
    t, d = h.shape
    rows = min(rows, t)
    row_spec = pl.BlockSpec((rows, d), lambda i: (i, 0))
    vec_spec = pl.BlockSpec((1, d), lambda i: (0, 0))
    return pl.pallas_call(
        _ln1_router_kernel,
        grid=(t // rows,),
        in_specs=[row_spec, row_spec, vec_spec, vec_spec,
                  pl.BlockSpec((d, LANES), lambda i: (0, 0)),
                  pl.BlockSpec((1, LANES), lambda i: (0, 0))],
        out_specs=[row_spec, pl.BlockSpec((rows, LANES), lambda i: (i, 0))],
        out_shape=[jax.ShapeDtypeStruct((t, d), F32), jax.ShapeDtypeStruct((t, LANES), F32)],
        compiler_params=_cparams(("parallel",)),
        name="ln1_router",
    )(h, mm, g.reshape(1, d), b.reshape(1, d), rw, rb)


def _matmul_kernel(a_ref, b_ref, o_ref):
    o_ref[...] = jnp.dot(a_ref[...], b_ref[...], preferred_element_type=F32)


def matmul(a, b, tm, tn):
    m, k = a.shape
    n = b.shape[1]
    tm = min(tm, m)
    return pl.pallas_call(
        _matmul_kernel,
        grid=(n // tn, m // tm),
        in_specs=[pl.BlockSpec((tm, k), lambda j, i: (i, 0)),
                  pl.BlockSpec((k, tn), lambda j, i: (0, j))],
        out_specs=pl.BlockSpec((tm, tn), lambda j, i: (i, j)),
        out_shape=jax.ShapeDtypeStruct((m, n), F32),
        compiler_params=_cparams(("parallel", "parallel")),
        name="matmul",
    )(a, b)


def _out_proj_kernel(a_ref, b_ref, c_ref, w_ref, o_ref):
    o_ref[...] = (jnp.dot(a_ref[...], w_ref[:A_WIDTH, :], preferred_element_type=F32)
                  + jnp.dot(b_ref[...], w_ref[A_WIDTH:A_WIDTH + B_WIDTH, :], preferred_element_type=F32)
                  + jnp.dot(c_ref[...], w_ref[A_WIDTH + B_WIDTH:, :], preferred_element_type=F32))


def out_proj(oa, ob, oc, w, tm, tn):
    m = oa.shape[0]
    k, n = w.shape
    tm = min(tm, m)
    return pl.pallas_call(
        _out_proj_kernel,
        grid=(n // tn, m // tm),
        in_specs=[pl.BlockSpec((tm, A_WIDTH), lambda j, i: (i, 0)),
                  pl.BlockSpec((tm, B_WIDTH), lambda j, i: (i, 0)),
                  pl.BlockSpec((tm, C_WIDTH), lambda j, i: (i, 0)),
                  pl.BlockSpec((k, tn), lambda j, i: (0, j))],
        out_specs=pl.BlockSpec((tm, tn), lambda j, i: (i, j)),
        out_shape=jax.ShapeDtypeStruct((m, n), F32),
        compiler_params=_cparams(("parallel", "parallel")),
        name="out_proj",
    )(oa, ob, oc, w)


def _hgrn_kernel(q_ref, f_ref, i_ref, g_ref, lb_ref, ng_ref, o_ref,
                 state_ref, cum_ref, kk_ref, acc_ref, *, n_chunks):
    @pl.when(pl.program_id(1) == 0)
    def _():
        state_ref[...] = jnp.zeros_like(state_ref)

    lb = lb_ref[...]
    ng = ng_ref[...]
    row = lax.broadcasted_iota(jnp.int32, (CHUNK, CHUNK), 0)
    col = lax.broadcasted_iota(jnp.int32, (CHUNK, CHUNK), 1)
    tri = (col <= row).astype(F32)
    sub = 8

    for c in range(n_chunks):
        r0 = c * CHUNK
        f = lb + (1.0 - lb) * jax.nn.sigmoid(f_ref[r0:r0 + CHUNK, :])
        kk = 1.0 - f
        cum = jnp.dot(tri, jnp.log(f), precision=HI, preferred_element_type=F32)
        qs = q_ref[r0:r0 + CHUNK, :] * (A_KEY ** -0.5)
        inter = lax.dot_general(qs * jnp.exp(cum), state_ref[...], (((1,), (1,)), ((), ())),
                                precision=HI, preferred_element_type=F32)
        cum_ref[...] = cum
        kk_ref[...] = kk
        acc_ref[...] = inter

        v = i_ref[r0:r0 + CHUNK, :]

        for sb in range(CHUNK // sub):
            t0 = sb * sub
            t1 = t0 + sub
            qs_d, cum_d = qs[t0:t1, :], cum[t0:t1, :]
            qs_b, cum_b = qs[t1:, :], cum[t1:, :]
            t_idx = t0 + lax.broadcasted_iota(jnp.int32, (sub, 1), 0)

            below = CHUNK > t1

            def body(s, acc, qs_d=qs_d, cum_d=cum_d, qs_b=qs_b, cum_b=cum_b, t_idx=t_idx, r0=r0, below=below):
                rc = cum_ref[pl.ds(s, 1), :]
                rk = kk_ref[pl.ds(s, 1), :]
                rv = i_ref[pl.ds(r0 + s, 1), :]
                a = jnp.sum(qs_d * rk * jnp.exp(jnp.minimum(cum_d - rc, 0.0)), axis=1, keepdims=True)
                acc_d = acc[0] + jnp.where(t_idx >= s, a, 0.0) * rv
                if not below:
                    return (acc_d,)
                return acc_d, acc[1] + jnp.sum(qs_b * rk * jnp.exp(cum_b - rc), axis=1, keepdims=True) * rv

            init = (jnp.zeros((sub, A_VAL), F32),) + ((jnp.zeros((CHUNK - t1, A_VAL), F32),) if below else ())
            acc = lax.fori_loop(t0, t1, body, init, unroll=True)
            acc_ref[t0:t1, :] += acc[0]
            if below:
                acc_ref[t1:, :] += acc[1]

        last = cum[CHUNK - 1:CHUNK, :]
        kd = kk * jnp.exp(last - cum)
        state_ref[...] = state_ref[...] * jnp.exp(last) + lax.dot_general(
            v, kd, (((0,), (0,)), ((), ())), precision=HI, preferred_element_type=F32)

        o = acc_ref[...]
        o = o * lax.rsqrt(jnp.mean(o * o, axis=-1, keepdims=True) + NORM_EPS) * ng
        g = g_ref[r0:r0 + CHUNK, :]
        o_ref[r0:r0 + CHUNK, :] = (o * (g * jax.nn.sigmoid(g))).astype(o_ref.dtype)


def hgrn_mixer(p, lb, ng, rows=256):
    t = p.shape[0]
    rows = min(rows, t)

    def seg(off):
        return pl.BlockSpec((rows, LANES), lambda h, r: (r, off + h))

    vec = pl.BlockSpec((1, LANES), lambda h, r: (0, h))
    return pl.pallas_call(
        functools.partial(_hgrn_kernel, n_chunks=rows // CHUNK),
        grid=(A_HEADS, t // rows),
        in_specs=[seg(OFF_AQ), seg(OFF_AF), seg(OFF_AI), seg(OFF_AG), vec, vec],
        out_specs=pl.BlockSpec((rows, LANES), lambda h, r: (r, h)),
        out_shape=jax.ShapeDtypeStruct((t, A_WIDTH), BF16),
        scratch_shapes=[pltpu.VMEM((A_VAL, A_KEY), F32), pltpu.VMEM((CHUNK, A_KEY), F32),
                        pltpu.VMEM((CHUNK, A_KEY), F32), pltpu.VMEM((CHUNK, A_VAL), F32)],
        compiler_params=_cparams(("parallel", "arbitrary")),
        name="hgrn_mixer",
    )(p, p, p, p, lb, ng)


def _ret_consts():
    lg = np.log(1.0 - np.exp(np.linspace(np.log(1.0 / 32), np.log(1.0 / 512), C_HEADS)))
    pos = np.arange(CHUNK, dtype=np.float64)
    rel = pos[:, None] - pos[None, :]
    dmat = np.where(rel >= 0, np.exp(lg[:, None, None] * np.maximum(rel, 0.0)), 0.0)
    q_dec = np.broadcast_to(np.exp(lg[:, None] * (pos + 1.0))[..., None], (C_HEADS, CHUNK, C_KEY))
    k_dec = np.broadcast_to(np.exp(lg[:, None] * (CHUNK - 1.0 - pos))[..., None], (C_HEADS, CHUNK, C_KEY))
    c_dec = np.broadcast_to(np.exp(lg * CHUNK)[:, None, None], (C_HEADS, 1, C_VAL))
    return tuple(jnp.asarray(np.ascontiguousarray(a), F32) for a in (dmat, q_dec, k_dec, c_dec))


def _ret_kernel(q_ref, k_ref, v_ref, g_ref, cos_ref, sin_ref, dm_ref, qd_ref, kd_ref, cd_ref, ng_ref,
                o_ref, state_ref, *, n_chunks):
    @pl.when(pl.program_id(1) == 0)
    def _():
        state_ref[...] = jnp.zeros_like(state_ref)

    dmat = dm_ref[...]
    q_dec = qd_ref[...]
    k_dec = kd_ref[...]
    c_dec = cd_ref[...]
    ng = ng_ref[...]
    for c in range(n_chunks):
        r0 = c * CHUNK
        cos = cos_ref[r0:r0 + CHUNK, :]
        sin = sin_ref[r0:r0 + CHUNK, :]
        q = q_ref[r0:r0 + CHUNK, :]
        k = k_ref[r0:r0 + CHUNK, :]
        q = q * cos + pltpu.roll(q, C_KEY // 2, 1) * sin
        k = (k * cos + pltpu.roll(k, C_KEY // 2, 1) * sin) * (C_KEY ** -0.5)
        v = v_ref[r0:r0 + CHUNK, :].astype(BF16)
        s = lax.dot_general(q.astype(BF16), k.astype(BF16), (((1,), (1,)), ((), ())), preferred_element_type=F32)
        intra = jnp.dot((s * dmat).astype(BF16), v, preferred_element_type=F32)
        inter = jnp.dot((q * q_dec).astype(BF16), state_ref[...].astype(BF16), preferred_element_type=F32)
        state_ref[...] = c_dec * state_ref[...] + jnp.dot(
            jnp.transpose(k * k_dec).astype(BF16), v, preferred_element_type=F32)
        o = intra + inter
        mu = jnp.mean(o, axis=-1, keepdims=True)
        oc = o - mu
        var = jnp.mean(oc * oc, axis=-1, keepdims=True)
        o = oc * lax.rsqrt(var + NORM_EPS) * ng
        g = g_ref[r0:r0 + CHUNK, :]
        o_ref[r0:r0 + CHUNK, :] = (o * (g * jax.nn.sigmoid(g))).astype(o_ref.dtype)


def retention_mixer(p, cos, sin, ng, rows=256):
    t = p.shape[0]
    rows = min(rows, t)
    dmat, q_dec, k_dec, c_dec = _ret_consts()
    tab = pl.BlockSpec((rows, C_KEY), lambda h, r: (r, 0))
    return pl.pallas_call(
        functools.partial(_ret_kernel, n_chunks=rows // CHUNK),
        grid=(C_HEADS, t // rows),
        in_specs=[pl.BlockSpec((rows, C_KEY), lambda h, r: (r, OFF_CQ + h)),
                  pl.BlockSpec((rows, C_KEY), lambda h, r: (r, OFF_CK + h)),
                  pl.BlockSpec((rows, C_VAL), lambda h, r: (r, OFF_CV // 2 + h)),
                  pl.BlockSpec((rows, C_VAL), lambda h, r: (r, OFF_CG // 2 + h)),
                  tab, tab,
                  pl.BlockSpec((None, CHUNK, CHUNK), lambda h, r: (h, 0, 0)),
                  pl.BlockSpec((None, CHUNK, C_KEY), lambda h, r: (h, 0, 0)),
                  pl.BlockSpec((None, CHUNK, C_KEY), lambda h, r: (h, 0, 0)),
                  pl.BlockSpec((None, 1, C_VAL), lambda h, r: (h, 0, 0)),
                  pl.BlockSpec((1, C_VAL), lambda h, r: (0, h))],
        out_specs=pl.BlockSpec((rows, C_VAL), lambda h, r: (r, h)),
        out_shape=jax.ShapeDtypeStruct((t, C_WIDTH), BF16),
        scratch_shapes=[pltpu.VMEM((C_KEY, C_VAL), F32)],
        compiler_params=_cparams(("parallel", "arbitrary")),
        name="retention_mixer",
    )(p, p, p, p, cos, sin, dmat, q_dec, k_dec, c_dec, ng)


def _rope_tables(t, n_rot, theta, width):
    half = n_rot // 2
    inv_freq = 1.0 / (theta ** (jnp.arange(0, n_rot, 2, dtype=F32) / n_rot))
    ang = jnp.arange(t, dtype=F32)[:, None] * inv_freq[None, :]
    c, s = jnp.cos(ang), jnp.sin(ang)
    z = jnp.zeros((t, width - n_rot), F32)
    zh = jnp.zeros((t, half), F32)
    cos = jnp.concatenate([c, c, jnp.ones((t, width - n_rot), F32)], axis=1)
    sin_lo = jnp.concatenate([-s, zh, z], axis=1)
    sin_hi = jnp.concatenate([zh, s, z], axis=1)
    rep = LANES // width
    return tuple(jnp.tile(a, (1, rep)) for a in (cos, sin_lo, sin_hi))


def _rope128(x, cos, sin_lo, sin_hi, half):
    return x * cos + pltpu.roll(x, LANES - half, 1) * sin_lo + pltpu.roll(x, half, 1) * sin_hi


def _dsa_prep_kernel(q_ref, k_ref, v_ref, iq_ref, ikw_ref, cb_ref, slb_ref, shb_ref, ci_ref, sli_ref, shi_ref,
                     qo_ref, ko_ref, vo_ref, iqo_ref, iko_ref, wo_ref):
    cb, slb, shb = cb_ref[...], slb_ref[...], shb_ref[...]
    ci, sli, shi = ci_ref[...], sli_ref[...], shi_ref[...]
    for h in range(B_HEADS):
        sl = slice(h * LANES, (h + 1) * LANES)
        qo_ref[:, sl] = _rope128(q_ref[:, sl], cb, slb, shb, B_ROT // 2).astype(BF16)
        ko_ref[:, sl] = _rope128(k_ref[:, sl], cb, slb, shb, B_ROT // 2).astype(BF16)
        iqo_ref[:, sl] = _rope128(iq_ref[:, sl], ci, sli, shi, IDX_ROT // 2).astype(BF16)
    vo_ref[...] = v_ref[...].astype(BF16)
    ikw = ikw_ref[...]
    iko_ref[...] = _rope128(ikw, ci, sli, shi, IDX_ROT // 2)[:, :IDX_DIM].astype(BF16)
    wo_ref[...] = ikw[:, IDX_DIM:IDX_DIM + IDX_HEADS] * (IDX_DIM ** -0.5 * IDX_HEADS ** -0.5)


def dsa_prep(p, rows=256):
    t = p.shape[0]
    rows = min(rows, t)
    tabs_b = _rope_tables(t, B_ROT, ROPE_THETA, B_HEAD_DIM)
    tabs_i = _rope_tables(t, IDX_ROT, ROPE_THETA, IDX_DIM)
    wide = lambda off: pl.BlockSpec((rows, B_WIDTH), lambda r: (r, off // 8))
    tab = pl.BlockSpec((rows, LANES), lambda r: (r, 0))
    out_wide = pl.BlockSpec((rows, B_WIDTH), lambda r: (r, 0))
    return pl.pallas_call(
        _dsa_prep_kernel,
        grid=(t // rows,),
        in_specs=[wide(OFF_BQ), wide(OFF_BK), wide(OFF_BV), wide(OFF_IQ),
                  pl.BlockSpec((rows, LANES), lambda r: (r, OFF_IKW))] + [tab] * 6,
        out_specs=[out_wide, out_wide, out_wide, out_wide,
                   pl.BlockSpec((rows, IDX_DIM), lambda r: (r, 0)),
                   pl.BlockSpec((rows, IDX_HEADS), lambda r: (r, 0))],
        out_shape=[jax.ShapeDtypeStruct((t, B_WIDTH), BF16)] * 4
        + [jax.ShapeDtypeStruct((t, IDX_DIM), BF16), jax.ShapeDtypeStruct((t, IDX_HEADS), F32)],
        compiler_params=_cparams(("parallel",)),
        name="dsa_prep",
    )(p, p, p, p, p, *tabs_b, *tabs_i)


def _dsa_kernel(q_ref, qi_ref, w_ref, kit_ref, k_ref, v_ref, o_ref,
                keys_ref, thr_ref, tie_ref, m_ref, l_ref, acc_ref, *, qb, kb, topk, pos_bits):
    i = pl.program_id(0)
    j = pl.program_id(1)
    nkb = ((i + 1) * qb + kb - 1) // kb

    @pl.when(j == 0)
    def _():
        q_pos = i * qb + lax.broadcasted_iota(jnp.int32, (qb, 1), 0)
        limit = (q_pos // CHUNK + 1) * CHUNK
        w = w_ref[...]

        def score_block(b, carry):
            off = pl.multiple_of(b * kb, kb)
            kit = kit_ref[:, pl.ds(off, kb)]
            sc = jnp.zeros((qb, kb), F32)
            for h in range(IDX_HEADS):
                lg = jnp.dot(qi_ref[:, h * IDX_DIM:(h + 1) * IDX_DIM], kit, preferred_element_type=F32)
                sc = sc + jnp.maximum(lg, 0.0) * w[:, h:h + 1]
            bits = pltpu.bitcast(sc, jnp.int32)
            key = jnp.where(bits < 0, bits ^ jnp.int32(0x7FFFFFFF), bits)
            k_pos = off + lax.broadcasted_iota(jnp.int32, (1, kb), 1)
            keys_ref[:, pl.ds(off, kb)] = jnp.where(k_pos < limit, key, jnp.int32(INT_MIN))
            return carry

        lax.fori_loop(0, nkb, score_block, 0)

        def count_rows(make_hit):
            cnts = []
            for g in range(qb // COUNT_ROWS):
                rows = slice(g * COUNT_ROWS, (g + 1) * COUNT_ROWS)
                hit = make_hit(rows)

                def count_block(b, cnt, rows=rows, hit=hit):
                    off = pl.multiple_of(b * kb, kb)
                    for c in range(kb // LANES):
                        blk = keys_ref[rows, pl.ds(off + c * LANES, LANES)]
                        cnt = cnt + hit(blk, off + c * LANES).astype(jnp.int32)
                    return cnt

                cnts.append(lax.fori_loop(0, nkb, count_block, jnp.zeros((COUNT_ROWS, LANES), jnp.int32)))
            return jnp.sum(jnp.concatenate(cnts, axis=0), axis=1, keepdims=True)

        def count_ge(cand):
            cand_b = jnp.broadcast_to(cand, (qb, LANES))
            return count_rows(lambda rows: (lambda blk, pos, c=cand_b[rows]: blk >= c))

        def bit_step(it, thr):
            cand = thr ^ lax.shift_left(jnp.int32(1), 31 - it)
            return jnp.where(count_ge(cand) >= topk, cand, thr)

        thr = lax.fori_loop(0, 32, bit_step, jnp.full((qb, 1), INT_MIN, jnp.int32))
        thr = jnp.maximum(thr, jnp.int32(INT_MIN + 1))
        thr_ref[...] = thr

        n_ge = count_ge(thr)
        tie_ref[...] = jnp.full_like(tie_ref, 2 ** 31 - 1)

        @pl.when(jnp.max(n_ge) > topk)
        def _():
            need = topk - count_ge(thr + 1)
            thr_b = jnp.broadcast_to(thr, (qb, LANES))
            lane = lax.broadcasted_iota(jnp.int32, (COUNT_ROWS, LANES), 1)

            def pos_step(it, x):
                cand = x | lax.shift_left(jnp.int32(1), pos_bits - 1 - it)
                cand_b = jnp.broadcast_to(cand, (qb, LANES))
                tied_before = count_rows(
                    lambda rows: (lambda blk, pos, th=thr_b[rows], c=cand_b[rows]: (blk == th) & (lane < c - pos)))
                return jnp.where(tied_before < need, cand, x)

            x = lax.fori_loop(0, pos_bits, pos_step, jnp.zeros((qb, 1), jnp.int32))
            tie_ref[...] = jnp.where(n_ge > topk, x, 2 ** 31 - 1)

        m_ref[...] = jnp.full_like(m_ref, -1e30)
        l_ref[...] = jnp.zeros_like(l_ref)
        acc_ref[...] = jnp.zeros_like(acc_ref)

    @pl.when(j < nkb)
    def _():
        off = pl.multiple_of(j * kb, kb)
        key = keys_ref[:, pl.ds(off, kb)]
        thr = thr_ref[...]
        k_pos = off + lax.broadcasted_iota(jnp.int32, (1, kb), 1)
        mask = (key >= thr) & ((key > thr) | (k_pos <= tie_ref[...]))
        for h in range(B_HEADS):
            sl = slice(h * B_HEAD_DIM, (h + 1) * B_HEAD_DIM)
            s = lax.dot_general(q_ref[:, sl], k_ref[:, sl], (((1,), (1,)), ((), ())),
                                preferred_element_type=F32) * (B_HEAD_DIM ** -0.5)
            s = jnp.where(mask, s, -1e30)
            m_old = m_ref[h]
            m_new = jnp.maximum(m_old, jnp.max(s, axis=1, keepdims=True))
            alpha = jnp.exp(m_old - m_new)
            p = jnp.exp(s - m_new)
            l_ref[h] = alpha * l_ref[h] + jnp.sum(p, axis=1, keepdims=True)
            acc_ref[:, sl] = alpha * acc_ref[:, sl] + jnp.dot(p.astype(BF16), v_ref[:, sl],
                                                              preferred_element_type=F32)
            m_ref[h] = m_new

    @pl.when(j == nkb - 1)
    def _():
        for h in range(B_HEADS):
            sl = slice(h * B_HEAD_DIM, (h + 1) * B_HEAD_DIM)
            o_ref[:, sl] = (acc_ref[:, sl] / l_ref[h]).astype(o_ref.dtype)


def dsa_mixer(p, qb=256, kb=1024):
    t = p.shape[0]
    qb = min(qb, t)
    kb = min(kb, t)
    topk = min(IDX_TOPK_MAX, t // 4)
    q, k, v, qi, ki, wi = dsa_prep(p)
    kit = ki.T

    def kv_map(i, j):
        return (jnp.minimum(j, ((i + 1) * qb + kb - 1) // kb - 1), 0)

    q_spec = pl.BlockSpec((qb, B_WIDTH), lambda i, j: (i, 0))
    return pl.pallas_call(
        functools.partial(_dsa_kernel, qb=qb, kb=kb, topk=topk, pos_bits=max(1, (t - 1).bit_length())),
        grid=(t // qb, t // kb),
        in_specs=[q_spec, q_spec,
                  pl.BlockSpec((qb, IDX_HEADS), lambda i, j: (i, 0)),
                  pl.BlockSpec((IDX_DIM, t), lambda i, j: (0, 0)),
                  pl.BlockSpec((kb, B_WIDTH), kv_map),
                  pl.BlockSpec((kb, B_WIDTH), kv_map)],
        out_specs=q_spec,
        out_shape=jax.ShapeDtypeStruct((t, B_WIDTH), BF16),
        scratch_shapes=[pltpu.VMEM((qb, t), jnp.int32), pltpu.VMEM((qb, 1), jnp.int32),
                        pltpu.VMEM((qb, 1), jnp.int32),
                        pltpu.VMEM((B_HEADS, qb, 1), F32), pltpu.VMEM((B_HEADS, qb, 1), F32),
                        pltpu.VMEM((qb, B_WIDTH), F32)],
        compiler_params=_cparams(("arbitrary", "arbitrary")),
        name="dsa_mixer",
    )(q, qi, wi, kit, k, v)


EXPERT_F_TILES = 3
DOWN_COLS = 1024
EXPERT_VMEM_LIMIT = 60 * 1024 * 1024


def _expert_kernel(be_ref, nu_ref, first_ref, nv_ref, tok_ref, h_hbm, wg_ref, wu_ref, bg_ref, bu_ref, wd_ref,
                   bd_ref, o_ref, xbuf_ref, xb_ref, act_ref, sem):
    b = pl.program_id(0)
    f = pl.program_id(1)
    n_used = nu_ref[0]
    used = b < n_used

    def row_copy(r, src_row):
        return pltpu.make_async_copy(h_hbm.at[pl.ds(src_row, 1), :], xbuf_ref.at[pl.ds(r, 1), :], sem)

    def issue_block(blk):
        first = first_ref[blk]
        last_valid = nv_ref[blk] - 1

        def issue(r, carry):
            row_copy(r, tok_ref[first + jnp.minimum(r, last_valid)]).start()
            return carry
        lax.fori_loop(0, MOE_BLOCK, issue, 0, unroll=8)

    @pl.when(jnp.logical_not(used) & (f == 0))
    def _():
        o_ref[...] = jnp.zeros_like(o_ref)

    @pl.when(used & (f == 0))
    def _():
        @pl.when(b == 0)
        def _():
            issue_block(0)

        def drain(r, carry):
            row_copy(r, 0).wait()
            return carry
        lax.fori_loop(0, MOE_BLOCK, drain, 0, unroll=8)
        xb_ref[...] = xbuf_ref[...].astype(BF16)

        @pl.when(b + 1 < n_used)
        def _():
            issue_block(b + 1)

    @pl.when(used)
    def _():
        x = xb_ref[...]
        g_lin = jnp.dot(x, wg_ref[...], preferred_element_type=F32) + bg_ref[...]
        u_lin = jnp.dot(x, wu_ref[...], preferred_element_type=F32) + bu_ref[...]
        g_lin = jnp.minimum(g_lin, SWIGLU_LIMIT)
        u_lin = jnp.clip(u_lin, -SWIGLU_LIMIT, SWIGLU_LIMIT)
        act = (u_lin + 1.0) * g_lin * jax.nn.sigmoid(SWIGLU_ALPHA * g_lin)
        ft = D_EXPERT // EXPERT_F_TILES
        tile = jnp.where(b % 2 == 0, f, EXPERT_F_TILES - 1 - f)
        act_ref[:, pl.ds(pl.multiple_of(tile * ft, ft), ft)] = act.astype(BF16)

        @pl.when(f == EXPERT_F_TILES - 1)
        def _():
            a = act_ref[...]
            for c in range(0, D_MODEL, DOWN_COLS):
                cols = slice(c, c + DOWN_COLS)
                o_ref[:, cols] = jnp.dot(a, wd_ref[:, cols], preferred_element_type=F32) + bd_ref[:, cols]


def expert_ffn(h, tok_sorted, block_expert, first, n_valid, n_used, layer, w_gu, b_gu, w_d, b_d):
    d = h.shape[1]
    n_blocks = block_expert.shape[0]
    ft = D_EXPERT // EXPERT_F_TILES

    def expert(b, be, nu):
        return be[jnp.minimum(b, nu[0] - 1)]

    def tile(b, f, nu):
        f = jnp.where(b < nu[0], f, EXPERT_F_TILES - 1)
        return jnp.where(jnp.minimum(b, nu[0] - 1) % 2 == 0, f, EXPERT_F_TILES - 1 - f)

    return pl.pallas_call(
        _expert_kernel,
        grid_spec=pltpu.PrefetchScalarGridSpec(
            num_scalar_prefetch=5,
            grid=(n_blocks, EXPERT_F_TILES),
            in_specs=[pl.BlockSpec(memory_space=pl.ANY),
                      pl.BlockSpec((None, None, d, ft),
                                   lambda b, f, be, nu, *_: (layer, expert(b, be, nu), 0, tile(b, f, nu))),
                      pl.BlockSpec((None, None, d, ft),
                                   lambda b, f, be, nu, *_: (layer, expert(b, be, nu), 0, EXPERT_F_TILES + tile(b, f, nu))),
                      pl.BlockSpec((None, None, 1, ft),
                                   lambda b, f, be, nu, *_: (layer, expert(b, be, nu), 0, tile(b, f, nu))),
                      pl.BlockSpec((None, None, 1, ft),
                                   lambda b, f, be, nu, *_: (layer, expert(b, be, nu), 0, EXPERT_F_TILES + tile(b, f, nu))),
                      pl.BlockSpec((None, None, D_EXPERT, d), lambda b, f, be, nu, *_: (layer, expert(b, be, nu), 0, 0)),
                      pl.BlockSpec((None, None, 1, d), lambda b, f, be, nu, *_: (layer, expert(b, be, nu), 0, 0))],
            out_specs=pl.BlockSpec((MOE_BLOCK, d), lambda b, f, *_: (b, 0)),
            scratch_shapes=[pltpu.VMEM((MOE_BLOCK, d), F32), pltpu.VMEM((MOE_BLOCK, d), BF16),
                            pltpu.VMEM((MOE_BLOCK, D_EXPERT), BF16), pltpu.SemaphoreType.DMA(())]),
        out_shape=jax.ShapeDtypeStruct((n_blocks * MOE_BLOCK, d), F32),
        compiler_params=_cparams(("arbitrary", "arbitrary"), EXPERT_VMEM_LIMIT),
        name="moe_experts",
    )(block_expert, n_used, first, n_valid, tok_sorted, h, w_gu, w_gu, b_gu, b_gu, w_d, b_d)


def _combine_ln_kernel(pos_ref, ys_hbm, gate_ref, h_ref, g_ref, b_ref, o_ref, ob_ref, buf_ref, sem, *, rows):
    i = pl.program_id(0)

    def row_copy(tile, r, k, src_row):
        slot = tile % 2
        return pltpu.make_async_copy(ys_hbm.at[pl.ds(src_row, 1), :], buf_ref.at[slot, k, pl.ds(r, 1), :],
                                     sem.at[slot])

    def issue_tile(tile):
        def issue(r, carry):
            for k in range(TOP_K):
                row_copy(tile, r, k, pos_ref[(tile * rows + r) * TOP_K + k]).start()
            return carry
        lax.fori_loop(0, rows, issue, 0, unroll=4)

    @pl.when(i == 0)
    def _():
        issue_tile(0)

    @pl.when(i + 1 < pl.num_programs(0))
    def _():
        issue_tile(i + 1)

    def drain(r, carry):
        for k in range(TOP_K):
            row_copy(i, r, k, 0).wait()
        return carry

    lax.fori_loop(0, rows, drain, 0)
    slot = i % 2
    gate = gate_ref[...]
    ffn = ((buf_ref[slot, 0] * gate[:, 0:1] + buf_ref[slot, 1] * gate[:, 1:2])
           + (buf_ref[slot, 2] * gate[:, 2:3] + buf_ref[slot, 3] * gate[:, 3:4]))
    y = _ln_rows(DN_ALPHA * h_ref[...] + ffn, g_ref[...], b_ref[...])
    o_ref[...] = y
    ob_ref[...] = y.astype(BF16)


def combine_ln(ys, pos, gates, h, g, b, rows=64):
    t, d = h.shape
    rows = min(rows, t)
    row_spec = pl.BlockSpec((rows, d), lambda i, pos: (i, 0))
    vec_spec = pl.BlockSpec((1, d), lambda i, pos: (0, 0))
    return pl.pallas_call(
        functools.partial(_combine_ln_kernel, rows=rows),
        grid_spec=pltpu.PrefetchScalarGridSpec(
            num_scalar_prefetch=1,
            grid=(t // rows,),
            in_specs=[pl.BlockSpec(memory_space=pl.ANY), pl.BlockSpec((rows, TOP_K), lambda i, pos: (i, 0)),
                      row_spec, vec_spec, vec_spec],
            out_specs=[row_spec, row_spec],
            scratch_shapes=[pltpu.VMEM((2, TOP_K, rows, d), F32), pltpu.SemaphoreType.DMA((2,))]),
        out_shape=[jax.ShapeDtypeStruct((t, d), F32), jax.ShapeDtypeStruct((t, d), BF16)],
        compiler_params=_cparams(("arbitrary",)),
        name="moe_combine_ln",
    )(pos, ys, gates, h, g.reshape(1, d), b.reshape(1, d))


def moe_routing(logits):
    n_tok = logits.shape[0]
    n_assign = n_tok * TOP_K
    n_blocks = -(-n_assign // MOE_BLOCK) + N_EXPERTS
    top_val, top_idx = lax.top_k(logits, TOP_K)
    gates = jax.nn.softmax(top_val, axis=-1)
    flat_e = top_idx.reshape(-1).astype(jnp.int32)
    iota = jnp.arange(n_assign, dtype=jnp.int32)
    se, order = lax.sort((flat_e, iota), num_keys=1, is_stable=True)
    experts = jnp.arange(N_EXPERTS, dtype=jnp.int32)
    counts = jnp.sum((flat_e[:, None] == experts[None, :]).astype(jnp.int32), axis=0)
    padded = (counts + MOE_BLOCK - 1) // MOE_BLOCK * MOE_BLOCK
    start = jnp.cumsum(counts) - counts
    pend = jnp.cumsum(padded)
    shift = pend - padded - start
    dest = iota + jnp.sum(jnp.where(se[:, None] == experts[None, :], shift[None, :], 0), axis=1)
    _, pos = lax.sort((order, dest), num_keys=1)
    blk = jnp.arange(n_blocks, dtype=jnp.int32)
    block_expert = jnp.minimum(
        jnp.sum((blk[:, None] * MOE_BLOCK >= pend[None, :]).astype(jnp.int32), axis=1), N_EXPERTS - 1)
    first = blk * MOE_BLOCK - shift[block_expert]
    n_valid = jnp.clip(start[block_expert] + counts[block_expert] - first, 0, MOE_BLOCK)
    n_used = (pend[-1:] // MOE_BLOCK).astype(jnp.int32)
    return order // TOP_K, gates, pos, block_expert, first, n_valid, n_used


def _pack_w_kernel(w_ref, o_ref):
    n_tail = w_ref.shape[1] - IKW_END
    ikw_at = IKW_START + n_tail
    o_ref[:, :IKW_START] = w_ref[:, :IKW_START].astype(BF16)
    o_ref[:, IKW_START:ikw_at] = w_ref[:, IKW_END:].astype(BF16)
    o_ref[:, ikw_at:ikw_at + IKW_END - IKW_START] = w_ref[:, IKW_START:IKW_END].astype(BF16)
    o_ref[:, ikw_at + IKW_END - IKW_START:] = jnp.zeros((o_ref.shape[0], PROJ_PAD - w_ref.shape[1]), BF16)


def pack_w_in(w_in, layer, rows=128):
    _, d, n = w_in.shape
    assert IKW_START + (n - IKW_END) == OFF_IKW * LANES
    return pl.pallas_call(
        _pack_w_kernel,
        grid=(d // rows,),
        in_specs=[pl.BlockSpec((None, rows, n), lambda i: (layer, i, 0))],
        out_specs=pl.BlockSpec((rows, PROJ_PAD), lambda i: (i, 0)),
        out_shape=jax.ShapeDtypeStruct((d, PROJ_PAD), BF16),
        compiler_params=_cparams(("parallel",)),
        name="pack_w_in",
    )(w_in)


def kernel(x, ln_in_g, ln_in_b, w_in, w_out, hgrn_lb, hgrn_norm_g, ret_norm_g, ln1_g, ln1_b,
           router_w, router_b, w_gate_up, b_gate_up, w_down, b_down, ln2_g, ln2_b):
    bsz, seq, d = x.shape
    assert bsz == 1
    t = seq
    lb_all = jnp.cumsum(jax.nn.softmax(hgrn_lb.astype(F32), axis=0), axis=0)
    lb_all = lb_all - lb_all[0:1]
    ang = jnp.arange(t, dtype=F32)[:, None] * (
        1.0 / (RET_THETA ** (jnp.arange(0, C_KEY, 2, dtype=F32) / C_KEY)))[None, :]
    cos_r = jnp.concatenate([jnp.cos(ang), jnp.cos(ang)], axis=1)
    sin_r = jnp.concatenate([-jnp.sin(ang), jnp.sin(ang)], axis=1)

    w_gu_b = w_gate_up.astype(BF16)
    w_d_b = w_down.astype(BF16)
    b_gu = b_gate_up.reshape(DEPTH, N_EXPERTS, 1, 2 * D_EXPERT)
    b_d = b_down.reshape(DEPTH, N_EXPERTS, 1, d)

    h, hb = ln_in(x.reshape(t, d), ln_in_g, ln_in_b)
    for l in range(DEPTH):
        p = matmul(hb, pack_w_in(w_in, l), tm=PROJ_ROWS, tn=9 * LANES)
        oa = hgrn_mixer(p, lb_all[l].reshape(1, A_WIDTH), hgrn_norm_g[l].reshape(1, A_WIDTH))
        ob = dsa_mixer(p)
        oc = retention_mixer(p, cos_r, sin_r, ret_norm_g[l].reshape(1, C_WIDTH))
        mm = out_proj(oa, ob, oc, w_out[l].astype(BF16), tm=PROJ_ROWS, tn=1024)
        rw = jnp.pad(router_w[l], ((0, 0), (0, LANES - N_EXPERTS))).astype(BF16)
        rb = jnp.pad(router_b[l], (0, LANES - N_EXPERTS)).reshape(1, LANES)
        h1, logits = ln1_router(h, mm, ln1_g[l], ln1_b[l], rw, rb)
        tok_sorted, gates, pos, block_expert, first, n_valid, n_used = moe_routing(logits[:, :N_EXPERTS])
        ys = expert_ffn(h1, tok_sorted, block_expert, first, n_valid, n_used, l, w_gu_b, b_gu, w_d_b, b_d)
        h, hb = combine_ln(ys, pos, gates, h1, ln2_g[l], ln2_b[l])
    return h.reshape(bsz, t, d)
```
